```python
import math
import jax, jax.numpy as jnp
from jax import lax
import numpy as np

D_MODEL = 1024
BATCH = 2
SEQ = 8192
DEPTH = 4

CONV_CH = D_MODEL // 2
CONV_WIDTH = 31
DIFF_HEADS = 4
DIFF_HEAD_DIM = (D_MODEL // 2) // DIFF_HEADS // 2
DIFF_QK = 2 * DIFF_HEADS * DIFF_HEAD_DIM
DIFF_V = DIFF_HEADS * 2 * DIFF_HEAD_DIM
ROPE_THETA = 500000.0
ROT_DIM = DIFF_HEAD_DIM // 4
GDN_HEADS = 4
GDN_HEAD_DIM = (D_MODEL // 2) // GDN_HEADS
GDN_W = GDN_HEADS * GDN_HEAD_DIM
GDN_CONV = 4
GDN_CHUNK = 64
FOX_HEADS = 4
FOX_HEAD_DIM = (D_MODEL // 2) // FOX_HEADS
FOX_W = FOX_HEADS * FOX_HEAD_DIM
Q_BLOCK = 128
D_FF = ((8 * D_MODEL + 3 * 256 - 1) // (3 * 256)) * 256
EVEN_IN = 2 * CONV_CH + 2 * DIFF_QK + DIFF_V
ODD_IN = 4 * GDN_W + 2 * GDN_HEADS + 3 * FOX_W + FOX_HEADS
MIX_W = CONV_CH + DIFF_V

kernel_name = 'hybrid_conformer_diffattn_gdn_fox'


def _split(t, sizes):
    return jnp.split(t, np.cumsum(sizes)[:-1].tolist(), axis=-1)


def rms_norm(x, w, eps=1e-6):
    xf = x.astype(jnp.float32)
    y = xf * lax.rsqrt(jnp.mean(xf * xf, axis=-1, keepdims=True) + eps)
    return (y * w.astype(jnp.float32)).astype(x.dtype)


def layer_norm(x, w, b, eps=1e-5):
    xf = x.astype(jnp.float32)
    mu = jnp.mean(xf, axis=-1, keepdims=True)
    xc = xf - mu
    var = jnp.mean(xc * xc, axis=-1, keepdims=True)
    return (xc * lax.rsqrt(var + eps) * w.astype(jnp.float32) + b.astype(jnp.float32)).astype(x.dtype)


def l2_normalize(x, eps=1e-6):
    xf = x.astype(jnp.float32)
    return xf * lax.rsqrt(jnp.sum(xf * xf, axis=-1, keepdims=True) + eps)


def causal_depthwise_conv(x, w):
    K, C = w.shape
    return lax.conv_general_dilated(
        x, w[:, None, :].astype(x.dtype), window_strides=(1,), padding=[(K - 1, 0)],
        dimension_numbers=('NWC', 'WIO', 'NWC'), feature_group_count=C)


def partial_rotary(x, pos):
    half = ROT_DIM // 2
    inv_freq = 1.0 / (ROPE_THETA ** (jnp.arange(half, dtype=jnp.float32) / half))
    ang = pos.astype(jnp.float32)[:, None] * inv_freq[None, :]
    cos = jnp.cos(ang)[None, :, None, :]
    sin = jnp.sin(ang)[None, :, None, :]
    x1 = x[..., :half].astype(jnp.float32)
    x2 = x[..., half:ROT_DIM].astype(jnp.float32)
    rot = jnp.concatenate([x1 * cos - x2 * sin, x2 * cos + x1 * sin], axis=-1)
    return jnp.concatenate([rot.astype(x.dtype), x[..., ROT_DIM:]], axis=-1)


def _block_mask(i, S):
    qpos = i * Q_BLOCK + jnp.arange(Q_BLOCK)
    return jnp.arange(S)[None, :] <= qpos[:, None]


def diff_attention(q, k, v, lam):
    B_, S, H2, Dh = q.shape
    H = H2 // 2
    nb = S // Q_BLOCK
    qb = q.reshape(B_, nb, Q_BLOCK, H2, Dh).transpose(1, 0, 3, 2, 4)
    scale = Dh ** -0.5

    def block(args):
        q_i, i = args
        s = jnp.einsum('bhqd,bkhd->bhqk', q_i, k).astype(jnp.float32) * scale
        s = jnp.where(_block_mask(i, S), s, -jnp.inf)
        p = jax.nn.softmax(s, axis=-1).reshape(B_, H, 2, Q_BLOCK, S)
        w = p[:, :, 0] - lam * p[:, :, 1]
        return jnp.einsum('bhqk,bkhe->bqhe', w.astype(v.dtype), v)

    o = lax.map(block, (qb, jnp.arange(nb)))
    return o.transpose(1, 0, 2, 3, 4).reshape(B_, S, H, 2 * Dh)


def forgetting_attention(q, k, v, log_f):
    B_, S, H, D = q.shape
    nb = S // Q_BLOCK
    c = jnp.cumsum(log_f, axis=1)
    qb = q.reshape(B_, nb, Q_BLOCK, H, D).transpose(1, 0, 3, 2, 4)
    cq = c.reshape(B_, nb, Q_BLOCK, H).transpose(1, 0, 3, 2)
    ck = c.transpose(0, 2, 1)
    scale = D ** -0.5

    def block(args):
        q_i, c_i, i = args
        s = jnp.einsum('bhqd,bkhd->bhqk', q_i, k).astype(jnp.float32) * scale
        s = s + c_i[..., None] - ck[:, :, None, :]
        s = jnp.where(_block_mask(i, S), s, -jnp.inf)
        p = jax.nn.softmax(s, axis=-1)
        return jnp.einsum('bhqk,bkhd->bqhd', p.astype(v.dtype), v)

    o = lax.map(block, (qb, cq, jnp.arange(nb)))
    return o.transpose(1, 0, 2, 3, 4).reshape(B_, S, H, D)


def gated_delta_rule(q, k, v, g, beta):
    f32 = jnp.float32
    B_, S, H, Dk = q.shape
    Dv = v.shape[-1]
    C = GDN_CHUNK
    nc = S // C

    def chunks(t):
        return t.astype(f32).reshape(B_, nc, C, H, -1).transpose(0, 3, 1, 2, 4)

    q = chunks(q) * (Dk ** -0.5)
    k = chunks(k)
    v = chunks(v)
    beta = chunks(beta[..., None])[..., 0]
    g = jnp.cumsum(chunks(g[..., None])[..., 0], axis=-1)
    idx = jnp.arange(C)
    causal = idx[:, None] >= idx[None, :]
    strict = idx[:, None] > idx[None, :]
    gdiff = g[..., :, None] - g[..., None, :]
    decay = jnp.where(causal, jnp.exp(jnp.where(causal, gdiff, 0.0)), 0.0)
    k_beta = k * beta[..., None]
    a_mat = jnp.where(strict, jnp.einsum('bhncd,bhned->bhnce', k_beta, k) * decay, 0.0)
    eye = jnp.eye(C, dtype=f32)
    t_inv = lax.linalg.triangular_solve(eye + a_mat, jnp.broadcast_to(eye, a_mat.shape),
                                        left_side=True, lower=True, unit_diagonal=True)
    u = jnp.einsum('bhnce,bhned->bhncd', t_inv, v * beta[..., None])
    w = jnp.einsum('bhnce,bhned->bhncd', t_inv, k_beta * jnp.exp(g)[..., None])
    qk = jnp.where(causal, jnp.einsum('bhncd,bhned->bhnce', q, k) * decay, 0.0)

    def step(state, xs):
        q_i, k_i, u_i, w_i, g_i, qk_i = xs
        v_new = u_i - jnp.einsum('bhcd,bhde->bhce', w_i, state)
        o_i = (jnp.einsum('bhcd,bhde->bhce', q_i * jnp.exp(g_i)[..., None], state)
               + jnp.einsum('bhce,bhed->bhcd', qk_i, v_new))
        g_last = g_i[..., -1:]
        state = (state * jnp.exp(g_last)[..., None]
                 + jnp.einsum('bhcd,bhce->bhde', k_i * jnp.exp(g_last - g_i)[..., None], v_new))
        return state, o_i

    xs = tuple(jnp.moveaxis(t, 2, 0) for t in (q, k, u, w, g, qk))
    state0 = jnp.zeros((B_, H, Dk, Dv), f32)
    _, o = lax.scan(step, state0, xs)
    return o.transpose(1, 0, 3, 2, 4).reshape(B_, S, H, Dv)


def even_mixer(h, pos, w_in, conv_w, conv_b, ln_w, ln_b, lq1, lk1, lq2, lk2, subln_w, lambda_init):
    B_, S, _ = h.shape
    a_val, a_gate, q, k, v = _split(h @ w_in, [CONV_CH, CONV_CH, DIFF_QK, DIFF_QK, DIFF_V])
    u = a_val * jax.nn.sigmoid(a_gate)
    u = causal_depthwise_conv(u, conv_w) + conv_b.astype(u.dtype)
    u = jax.nn.silu(layer_norm(u, ln_w, ln_b))
    q = partial_rotary(q.reshape(B_, S, 2 * DIFF_HEADS, DIFF_HEAD_DIM), pos)
    k = partial_rotary(k.reshape(B_, S, 2 * DIFF_HEADS, DIFF_HEAD_DIM), pos)
    v = v.reshape(B_, S, DIFF_HEADS, 2 * DIFF_HEAD_DIM)
    f32 = jnp.float32
    lam = (jnp.exp(jnp.sum(lq1.astype(f32) * lk1.astype(f32)))
           - jnp.exp(jnp.sum(lq2.astype(f32) * lk2.astype(f32))) + lambda_init)
    o = diff_attention(q, k, v, lam)
    o = rms_norm(o, subln_w) * (1.0 - lambda_init)
    return jnp.concatenate([u, o.reshape(B_, S, DIFF_V)], axis=-1)


def odd_mixer(h, sconv_w, w_in, a_log, dt_bias, gnorm_w, f_bias):
    B_, S, _ = h.shape
    f32 = jnp.float32
    qkv_c, z_c, b_c, a_c, q_d, k_d, v_d, f_d = _split(
        h @ w_in, [3 * GDN_W, GDN_W, GDN_HEADS, GDN_HEADS, FOX_W, FOX_W, FOX_W, FOX_HEADS])
    qkv_c = jax.nn.silu(causal_depthwise_conv(qkv_c, sconv_w))
    q_c, k_c, v_c = [t.reshape(B_, S, GDN_HEADS, GDN_HEAD_DIM) for t in _split(qkv_c, [GDN_W, GDN_W, GDN_W])]
    beta = jax.nn.sigmoid(b_c.astype(f32))
    g = -jnp.exp(a_log.astype(f32)) * jax.nn.softplus(a_c.astype(f32) + dt_bias.astype(f32))
    o_c = gated_delta_rule(l2_normalize(q_c), l2_normalize(k_c), v_c, g, beta).astype(h.dtype)
    o_c = rms_norm(o_c, gnorm_w) * jax.nn.silu(z_c.reshape(B_, S, GDN_HEADS, GDN_HEAD_DIM))
    log_f = jax.nn.log_sigmoid(f_d.astype(f32) + f_bias.astype(f32))
    o_d = forgetting_attention(q_d.reshape(B_, S, FOX_HEADS, FOX_HEAD_DIM),
                               k_d.reshape(B_, S, FOX_HEADS, FOX_HEAD_DIM),
                               v_d.reshape(B_, S, FOX_HEADS, FOX_HEAD_DIM), log_f)
    return jnp.concatenate([o_c.reshape(B_, S, GDN_W), o_d.reshape(B_, S, FOX_W)], axis=-1)


def setup_inputs(seed: int = 0) -> dict:
    key = jax.random.key(seed)
    ks = jax.random.split(key, 32)
    n_even = (DEPTH + 1) // 2
    n_odd = DEPTH // 2
    f32 = jnp.float32

    def nrm(k, shape, scale):
        return jax.random.normal(k, shape, f32) * scale

    dt = jnp.exp(jax.random.uniform(ks[17], (n_odd, GDN_HEADS), f32, math.log(1e-3), math.log(1e-1)))
    return {
        'x': nrm(ks[0], (BATCH, SEQ, D_MODEL), 1.0),
        'mix_norm_w': 1.0 + nrm(ks[1], (DEPTH, D_MODEL), 0.02),
        'ffn_norm_w': 1.0 + nrm(ks[2], (DEPTH, D_MODEL), 0.02),
        'final_norm_w': 1.0 + nrm(ks[3], (D_MODEL,), 0.02),
        'w_in_even': nrm(ks[4], (n_even, D_MODEL, EVEN_IN), D_MODEL ** -0.5),
        'conv_w': nrm(ks[5], (n_even, CONV_WIDTH, CONV_CH), CONV_WIDTH ** -0.5),
        'conv_b': nrm(ks[6], (n_even, CONV_CH), 0.02),
        'conv_ln_w': 1.0 + nrm(ks[7], (n_even, CONV_CH), 0.02),
        'conv_ln_b': nrm(ks[8], (n_even, CONV_CH), 0.02),
        'lambda_q1': nrm(ks[9], (n_even, DIFF_HEAD_DIM), 0.1),
        'lambda_k1': nrm(ks[10], (n_even, DIFF_HEAD_DIM), 0.1),
        'lambda_q2': nrm(ks[11], (n_even, DIFF_HEAD_DIM), 0.1),
        'lambda_k2': nrm(ks[12], (n_even, DIFF_HEAD_DIM), 0.1),
        'diff_subln_w': 1.0 + nrm(ks[13], (n_even, 2 * DIFF_HEAD_DIM), 0.02),
        'gdn_conv_w': nrm(ks[14], (n_odd, GDN_CONV, 3 * GDN_W), GDN_CONV ** -0.5),
        'w_in_odd': nrm(ks[15], (n_odd, D_MODEL, ODD_IN), D_MODEL ** -0.5),
        'gdn_a_log': jnp.log(jax.random.uniform(ks[16], (n_odd, GDN_HEADS), f32, 1.0, 16.0)),
        'gdn_dt_bias': dt + jnp.log(-jnp.expm1(-dt)),
        'gdn_norm_w': 1.0 + nrm(ks[18], (n_odd, GDN_HEAD_DIM), 0.02),
        'fox_forget_bias': nrm(ks[19], (n_odd, FOX_HEADS), 0.1),
        'w_out': nrm(ks[20], (DEPTH, MIX_W, D_MODEL), MIX_W ** -0.5),
        'w_gate': nrm(ks[21], (DEPTH, D_MODEL, D_FF), D_MODEL ** -0.5),
        'w_up': nrm(ks[22], (DEPTH, D_MODEL, D_FF), D_MODEL ** -0.5),
        'w_down': nrm(ks[23], (DEPTH, D_FF, D_MODEL), D_FF ** -0.5),
    }


def reference(x, mix_norm_w, ffn_norm_w, final_norm_w, w_in_even, conv_w, conv_b, conv_ln_w, conv_ln_b,
              lambda_q1, lambda_k1, lambda_q2, lambda_k2, diff_subln_w, gdn_conv_w, w_in_odd,
              gdn_a_log, gdn_dt_bias, gdn_norm_w, fox_forget_bias, w_out, w_gate, w_up, w_down):
    pos = jnp.arange(x.shape[1])
    for l in range(DEPTH):
        i = l // 2
        h = rms_norm(x, mix_norm_w[l])
        if l % 2 == 0:
            lambda_init = 0.8 - 0.6 * math.exp(-0.3 * l)
            y = even_mixer(h, pos, w_in_even[i], conv_w[i], conv_b[i], conv_ln_w[i], conv_ln_b[i],
                           lambda_q1[i], lambda_k1[i], lambda_q2[i], lambda_k2[i], diff_subln_w[i],
                           lambda_init)
        else:
            y = odd_mixer(h, gdn_conv_w[i], w_in_odd[i], gdn_a_log[i], gdn_dt_bias[i], gdn_norm_w[i],
                          fox_forget_bias[i])
        x = x + y @ w_out[l]
        h = rms_norm(x, ffn_norm_w[l])
        x = x + (jax.nn.silu(h @ w_gate[l]) * (h @ w_up[l])) @ w_down[l]
    return rms_norm(x, final_norm_w)
```

```python
import functools
import math

import jax
import jax.numpy as jnp
from jax import lax
from jax.experimental import pallas as pl
from jax.experimental.pallas import tpu as pltpu

F32 = jnp.float32
BF16 = jnp.bfloat16
HIGHEST = lax.Precision.HIGHEST

D_MODEL = 1024
DEPTH = 4
CONV_CH = 512
CONV_WIDTH = 31
DIFF_HEADS = 4
DIFF_HEAD_DIM = 64
ROPE_THETA = 500000.0
ROT_DIM = 16
GDN_HEADS = 4
GDN_HEAD_DIM = 128
GDN_W = 512
GDN_CONV = 4
FOX_HEADS = 4
FOX_W = 512
D_FF = 2816
LANE = 128
GDN_CHUNK = 128
NEG_BIG = -1e30

TM_PROJ = 512
TM_FFN = 512
T_CONV = 512
CONV_HALO = 32
CONV_ROWS = 32
TQ_DIFF = 256
TQ_FOX = 512
VMEM_LIMIT = 56 * 1024 * 1024


def _cparams(sem):
    return pltpu.CompilerParams(dimension_semantics=sem, vmem_limit_bytes=VMEM_LIMIT)


def _rms(x, w, eps=1e-6):
    return x * lax.rsqrt(jnp.mean(x * x, axis=-1, keepdims=True) + eps) * w


def _sigmoid(x):
    return 1.0 / (1.0 + jnp.exp(-x))


def _silu(x):
    return x * _sigmoid(x)


def _softplus(x):
    return jnp.maximum(x, 0.0) + jnp.log(1.0 + jnp.exp(-jnp.abs(x)))


def _dot(a, b):
    return jnp.dot(a, b, preferred_element_type=F32)


def _dot_nt(a, b):
    return lax.dot_general(a, b, (((1,), (1,)), ((), ())), preferred_element_type=F32)


def _dot_hi(a, b):
    return jnp.dot(a, b, preferred_element_type=F32, precision=HIGHEST)


def _even_inproj_kernel(x_ref, nw_ref, w_ref, c_ref, sa_ref, sb_ref, conv_ref, qkv_ref):
    hb = _rms(x_ref[...], nw_ref[...]).astype(BF16)
    conv_ref[...] = _dot(hb, w_ref[:, :2 * CONV_CH])
    qk = _dot(hb, w_ref[:, 2 * CONV_CH:2 * CONV_CH + 1024])
    c = c_ref[...]
    sa = sa_ref[...]
    sb = sb_ref[...]
    for j in range(8):
        blk = qk[:, LANE * j:LANE * (j + 1)]
        rot = blk * c + pltpu.roll(blk, LANE - ROT_DIM // 2, 1) * sa + pltpu.roll(blk, ROT_DIM // 2, 1) * sb
        if j < 4:
            rot = rot * (DIFF_HEAD_DIM ** -0.5)
        qkv_ref[:, LANE * j:LANE * (j + 1)] = rot.astype(BF16)
    qkv_ref[:, 1024:] = _dot(hb, w_ref[:, 2 * CONV_CH + 1024:]).astype(BF16)


def _even_inproj(x, nw, w, rot_c, rot_sa, rot_sb, seq):
    m = x.shape[0]
    tm = min(TM_PROJ, seq)
    nrot = seq // tm
    n_in = w.shape[1]
    return pl.pallas_call(
        _even_inproj_kernel,
        grid=(m // tm,),
        in_specs=[
            pl.BlockSpec((tm, D_MODEL), lambda i: (i, 0)),
            pl.BlockSpec((1, D_MODEL), lambda i: (0, 0)),
            pl.BlockSpec((D_MODEL, n_in), lambda i: (0, 0), pipeline_mode=pl.Buffered(1)),
            pl.BlockSpec((tm, LANE), lambda i: (i % nrot, 0)),
            pl.BlockSpec((tm, LANE), lambda i: (i % nrot, 0)),
            pl.BlockSpec((tm, LANE), lambda i: (i % nrot, 0)),
        ],
        out_specs=[
            pl.BlockSpec((tm, 2 * CONV_CH), lambda i: (i, 0)),
            pl.BlockSpec((tm, 1536), lambda i: (i, 0)),
        ],
        out_shape=[
            jax.ShapeDtypeStruct((m, 2 * CONV_CH), F32),
            jax.ShapeDtypeStruct((m, 1536), BF16),
        ],
        compiler_params=_cparams(("parallel",)),
        name="even_inproj",
    )(x, nw, w, rot_c, rot_sa, rot_sb)


def _odd_inproj_kernel(x_ref, nw_ref, w_ref, gdn_ref, small_ref, fox_ref):
    hb = _rms(x_ref[...], nw_ref[...]).astype(BF16)
    gdn_ref[...] = _dot(hb, w_ref[:, :2048])
    y = _dot(hb, w_ref[:, 2048:3584])
    fox_ref[:, :FOX_W] = (y[:, :FOX_W] * (GDN_HEAD_DIM ** -0.5)).astype(BF16)
    fox_ref[:, FOX_W:] = y[:, FOX_W:].astype(BF16)
    small_ref[...] = _dot(hb, w_ref[:, 3584:])


def _odd_inproj(x, nw, w, seq):
    m = x.shape[0]
    tm = min(TM_PROJ, seq)
    n_in = w.shape[1]
    return pl.pallas_call(
        _odd_inproj_kernel,
        grid=(m // tm,),
        in_specs=[
            pl.BlockSpec((tm, D_MODEL), lambda i: (i, 0)),
            pl.BlockSpec((1, D_MODEL), lambda i: (0, 0)),
            pl.BlockSpec((D_MODEL, n_in), lambda i: (0, 0), pipeline_mode=pl.Buffered(1)),
        ],
        out_specs=[
            pl.BlockSpec((tm, 2048), lambda i: (i, 0)),
            pl.BlockSpec((tm, LANE), lambda i: (i, 0)),
            pl.BlockSpec((tm, 1536), lambda i: (i, 0)),
        ],
        out_shape=[
            jax.ShapeDtypeStruct((m, 2048), F32),
            jax.ShapeDtypeStruct((m, LANE), F32),
            jax.ShapeDtypeStruct((m, 1536), BF16),
        ],
        compiler_params=_cparams(("parallel",)),
        name="odd_inproj",
    )(x, nw, w)


def _conv_kernel(val_ref, gate_ref, pval_ref, pgate_ref, cw_ref, cb_ref, lw_ref, lb_ref, o_ref, buf_ref, *, t):
    i = pl.program_id(1)
    pu = pval_ref[...] * _sigmoid(pgate_ref[...])
    buf_ref[0:CONV_HALO, :] = jnp.where(i > 0, pu, 0.0)
    buf_ref[CONV_HALO:CONV_HALO + t, :] = val_ref[...] * _sigmoid(gate_ref[...])
    off = CONV_HALO - (CONV_WIDTH - 1)
    cb = cb_ref[...]
    lw = lw_ref[...]
    lb = lb_ref[...]
    for r in range(t // CONV_ROWS):
        r0 = r * CONV_ROWS
        acc = jnp.broadcast_to(cb, (CONV_ROWS, CONV_CH))
        for j in range(CONV_WIDTH):
            acc = acc + cw_ref[j:j + 1, :] * buf_ref[r0 + off + j:r0 + off + j + CONV_ROWS, :]
        mu = jnp.mean(acc, axis=-1, keepdims=True)
        xc = acc - mu
        var = jnp.mean(xc * xc, axis=-1, keepdims=True)
        y = xc * lax.rsqrt(var + 1e-5) * lw + lb
        o_ref[r0:r0 + CONV_ROWS, :] = _silu(y).astype(o_ref.dtype)


def _conv_module(conv_in, cw, cb, lw, lb, batch, seq):
    m = conv_in.shape[0]
    t = min(T_CONV, seq)
    nt = seq // t
    hb = t // CONV_HALO

    def prev_map(col):
        return lambda b, i: (jnp.maximum((b * nt + i) * hb - 1, 0), col)

    return pl.pallas_call(
        functools.partial(_conv_kernel, t=t),
        grid=(batch, nt),
        in_specs=[
            pl.BlockSpec((t, CONV_CH), lambda b, i: (b * nt + i, 0)),
            pl.BlockSpec((t, CONV_CH), lambda b, i: (b * nt + i, 1)),
            pl.BlockSpec((CONV_HALO, CONV_CH), prev_map(0)),
            pl.BlockSpec((CONV_HALO, CONV_CH), prev_map(1)),
            pl.BlockSpec((CONV_HALO, CONV_CH), lambda b, i: (0, 0)),
            pl.BlockSpec((1, CONV_CH), lambda b, i: (0, 0)),
            pl.BlockSpec((1, CONV_CH), lambda b, i: (0, 0)),
            pl.BlockSpec((1, CONV_CH), lambda b, i: (0, 0)),
        ],
        out_specs=pl.BlockSpec((t, CONV_CH), lambda b, i: (b * nt + i, 0)),
        out_shape=jax.ShapeDtypeStruct((m, CONV_CH), BF16),
        scratch_shapes=[pltpu.VMEM((CONV_HALO + t, CONV_CH), F32)],
        compiler_params=_cparams(("parallel", "parallel")),
        name="conv_module",
    )(conv_in, conv_in, conv_in, conv_in, cw, cb, lw, lb)


def _flash(q, k_ref, v_ref, qi, tq, nrep, row_bias, col_bias_fn):
    rows = nrep * tq

    def step(j, carry, diag):
        m, l, acc = carry
        start = pl.multiple_of(j * tq, tq)
        k = k_ref[pl.ds(start, tq), :]
        v = v_ref[pl.ds(start, tq), :]
        s = _dot_nt(q, k)
        if row_bias is not None:
            s = s + row_bias + col_bias_fn(start)
        if diag:
            rpos = lax.broadcasted_iota(jnp.int32, (rows, tq), 0)
            if nrep > 1:
                rpos = jnp.where(rpos >= tq, rpos - tq, rpos)
            cpos = lax.broadcasted_iota(jnp.int32, (rows, tq), 1)
            s = jnp.where(cpos <= rpos, s, NEG_BIG)
        m_new = jnp.maximum(m, jnp.max(s, axis=-1, keepdims=True))
        alpha = jnp.exp(m - m_new)
        p = jnp.exp(s - m_new)
        l = alpha * l + jnp.sum(p, axis=-1, keepdims=True)
        acc = alpha * acc + _dot(p.astype(BF16), v)
        return m_new, l, acc

    init = (jnp.full((rows, 1), NEG_BIG, F32), jnp.zeros((rows, 1), F32), jnp.zeros((rows, LANE), F32))
    carry = lax.fori_loop(0, qi, lambda j, c: step(j, c, False), init)
    _, l, acc = step(qi, carry, True)
    return acc, l


def _diff_attn_kernel(q_ref, k_ref, v_ref, lq1_ref, lk1_ref, lq2_ref, lk2_ref, sw_ref, o_ref, *, tq, lambda_init):
    qi = pl.program_id(2)
    q = q_ref[...]
    lane = lax.broadcasted_iota(jnp.int32, q.shape, 1)
    zero = jnp.zeros_like(q)
    qs = jnp.concatenate([jnp.where(lane < DIFF_HEAD_DIM, q, zero), jnp.where(lane >= DIFF_HEAD_DIM, q, zero)], axis=0)
    acc, l = _flash(qs, k_ref, v_ref, qi, tq, 2, None, None)
    lam = (jnp.exp(jnp.sum(lq1_ref[...] * lk1_ref[...], axis=-1, keepdims=True))
           - jnp.exp(jnp.sum(lq2_ref[...] * lk2_ref[...], axis=-1, keepdims=True)) + lambda_init)
    o = acc[:tq] / l[:tq] - lam * (acc[tq:] / l[tq:])
    o_ref[...] = (_rms(o, sw_ref[...]) * (1.0 - lambda_init)).astype(o_ref.dtype)


def _diff_attention(qkv, lq1, lk1, lq2, lk2, sw, lambda_init, batch, seq):
    m = qkv.shape[0]
    tq = min(TQ_DIFF, seq)
    nq = seq // tq
    vec = pl.BlockSpec((1, DIFF_HEAD_DIM), lambda b, h, i: (0, 0))
    return pl.pallas_call(
        functools.partial(_diff_attn_kernel, tq=tq, lambda_init=lambda_init),
        grid=(batch, DIFF_HEADS, nq),
        in_specs=[
            pl.BlockSpec((tq, LANE), lambda b, h, i: (b * nq + i, h)),
            pl.BlockSpec((seq, LANE), lambda b, h, i: (b, 4 + h)),
            pl.BlockSpec((seq, LANE), lambda b, h, i: (b, 8 + h)),
            vec, vec, vec, vec,
            pl.BlockSpec((1, LANE), lambda b, h, i: (0, 0)),
        ],
        out_specs=pl.BlockSpec((tq, LANE), lambda b, h, i: (b * nq + i, h)),
        out_shape=jax.ShapeDtypeStruct((m, 4 * LANE), BF16),
        compiler_params=_cparams(("parallel", "parallel", "parallel")),
        name="diff_attention",
    )(qkv, qkv, qkv, lq1, lk1, lq2, lk2, sw)


def _fox_attn_kernel(q_ref, k_ref, v_ref, cq_ref, ck_ref, o_ref, kbias_ref, *, tq):
    h = pl.program_id(1)
    qi = pl.program_id(2)
    cq_blk = cq_ref[...]
    lane = lax.broadcasted_iota(jnp.int32, cq_blk.shape, 1)
    cq = jnp.sum(jnp.where(lane == h, cq_blk, 0.0), axis=-1, keepdims=True)
    ck_blk = ck_ref[0]
    sub = lax.broadcasted_iota(jnp.int32, ck_blk.shape, 0)
    ck = jnp.sum(jnp.where(sub == h, ck_blk, 0.0), axis=0, keepdims=True)
    c_base = cq[0:1, :]
    kbias_ref[...] = c_base - ck

    def col_bias(start):
        return kbias_ref[:, pl.ds(start, tq)]

    acc, l = _flash(q_ref[...], k_ref, v_ref, qi, tq, 1, cq - c_base, col_bias)
    o_ref[...] = (acc / l).astype(o_ref.dtype)


def _fox_attention(fox, c_col, c_row, batch, seq):
    m = fox.shape[0]
    tq = min(TQ_FOX, seq)
    nq = seq // tq
    return pl.pallas_call(
        functools.partial(_fox_attn_kernel, tq=tq),
        grid=(batch, FOX_HEADS, nq),
        in_specs=[
            pl.BlockSpec((tq, LANE), lambda b, h, i: (b * nq + i, h)),
            pl.BlockSpec((seq, LANE), lambda b, h, i: (b, 4 + h)),
            pl.BlockSpec((seq, LANE), lambda b, h, i: (b, 8 + h)),
            pl.BlockSpec((tq, LANE), lambda b, h, i: (b * nq + i, 0)),
            pl.BlockSpec((1, 8, seq), lambda b, h, i: (b, 0, 0)),
        ],
        out_specs=pl.BlockSpec((tq, LANE), lambda b, h, i: (b * nq + i, h)),
        out_shape=jax.ShapeDtypeStruct((m, FOX_W), BF16),
        scratch_shapes=[pltpu.VMEM((1, seq), F32)],
        compiler_params=_cparams(("parallel", "parallel", "parallel")),
        name="fox_attention",
    )(fox, fox, fox, c_col, c_row)


SM_F, SM_B, SM_A = 0, 4, 8


def _unit_lower_inverse(a, row, col):
    eye = (row == col).astype(F32)
    shift = 4
    same = jnp.right_shift(row, shift) == jnp.right_shift(col, shift)
    n1 = jnp.where(same, a, 0.0)
    n2 = _dot_hi(n1, n1)
    n4 = _dot_hi(n2, n2)
    n8 = _dot_hi(n4, n4)
    x = eye - n1
    x = x + _dot_hi(x, n2)
    x = x + _dot_hi(x, n4)
    x = x + _dot_hi(x, n8)
    while (1 << shift) < a.shape[0]:
        shift += 1
        wider = jnp.right_shift(row, shift) == jnp.right_shift(col, shift)
        r = jnp.where(wider, jnp.where(same, 0.0, a), 0.0)
        x = x - _dot_hi(_dot_hi(x, r), x)
        same = wider
    return x


def _gdn_kernel(qkv_ref, prev_ref, z_ref, small_ref, cw_ref, alog_ref, dt_ref, fb_ref, gw_ref,
                o_ref, ccol_ref, crow_ref, state_ref, carry_ref, buf_ref):
    i = pl.program_id(1)
    c = GDN_CHUNK

    @pl.when(i == 0)
    def _():
        state_ref[...] = jnp.zeros_like(state_ref)
        carry_ref[...] = jnp.zeros_like(carry_ref)

    buf_ref[0:8, :] = jnp.where(i > 0, prev_ref[...], 0.0)
    buf_ref[8:8 + c, :] = qkv_ref[...]
    off = 8 - (GDN_CONV - 1)
    y = cw_ref[0:1, :] * buf_ref[off:off + c, :]
    for j in range(1, GDN_CONV):
        y = y + cw_ref[j:j + 1, :] * buf_ref[off + j:off + j + c, :]
    y = _silu(y)

    row = lax.broadcasted_iota(jnp.int32, (c, c), 0)
    col = lax.broadcasted_iota(jnp.int32, (c, c), 1)
    causal = row >= col
    strict = row > col
    ltri = causal.astype(F32)

    sm = small_ref[...]
    xf = sm + fb_ref[...]
    logf = jnp.minimum(xf, 0.0) - jnp.log(1.0 + jnp.exp(-jnp.abs(xf)))
    cl = _dot_hi(ltri, logf) + carry_ref[...]
    carry_ref[...] = cl[c - 1:c, :]
    ccol_ref[...] = cl
    crow_ref[0] = cl.T[0:8, :]

    g = -jnp.exp(alog_ref[...]) * _softplus(sm + dt_ref[...])
    gc = _dot_hi(ltri, g)
    gct = gc.T
    beta_all = _sigmoid(sm)
    gw = gw_ref[...]

    for h in range(GDN_HEADS):
        qh = y[:, h * LANE:(h + 1) * LANE]
        kh = y[:, GDN_W + h * LANE:GDN_W + (h + 1) * LANE]
        vh = y[:, 2 * GDN_W + h * LANE:2 * GDN_W + (h + 1) * LANE]
        qn = qh * (lax.rsqrt(jnp.sum(qh * qh, axis=-1, keepdims=True) + 1e-6) * (GDN_HEAD_DIM ** -0.5))
        kn = kh * lax.rsqrt(jnp.sum(kh * kh, axis=-1, keepdims=True) + 1e-6)
        gcol = gc[:, SM_A + h:SM_A + h + 1]
        grow = gct[SM_A + h:SM_A + h + 1, :]
        beta = beta_all[:, SM_B + h:SM_B + h + 1]
        decay = jnp.where(causal, jnp.exp(jnp.where(causal, gcol - grow, 0.0)), 0.0)
        kb = kn * beta
        knb = kn.astype(BF16)
        a_mat = jnp.where(strict, _dot_nt(kb.astype(BF16), knb) * decay, 0.0)
        t_inv = _unit_lower_inverse(a_mat, row, col).astype(BF16)
        eg = jnp.exp(gcol)
        uw = _dot(t_inv, jnp.concatenate([vh * beta, kb * eg], axis=-1).astype(BF16))
        u = uw[:, :LANE]
        w = uw[:, LANE:]
        qk = jnp.where(causal, _dot_nt(qn.astype(BF16), knb) * decay, 0.0)
        s = state_ref[h]
        sb = s.astype(BF16)
        v_new = u - _dot(w.astype(BF16), sb)
        vnb = v_new.astype(BF16)
        o = _dot((qn * eg).astype(BF16), sb) + _dot(qk.astype(BF16), vnb)
        g_last = gcol[c - 1:c, :]
        kd = kn * jnp.exp(g_last - gcol)
        state_ref[h] = s * jnp.exp(g_last) + _dot(kd.T.astype(BF16), vnb)
        zh = z_ref[:, h * LANE:(h + 1) * LANE]
        o_ref[:, h * LANE:(h + 1) * LANE] = (_rms(o, gw) * _silu(zh)).astype(o_ref.dtype)


def _gdn(gdn_in, small, cw, alog, dt, fb, gw, batch, seq):
    m = gdn_in.shape[0]
    c = GDN_CHUNK
    nc = seq // c
    vec = pl.BlockSpec((1, LANE), lambda b, i: (0, 0))
    return pl.pallas_call(
        _gdn_kernel,
        grid=(batch, nc),
        in_specs=[
            pl.BlockSpec((c, 3 * GDN_W), lambda b, i: (b * nc + i, 0)),
            pl.BlockSpec((8, 3 * GDN_W), lambda b, i: (jnp.maximum((b * nc + i) * (c // 8) - 1, 0), 0)),
            pl.BlockSpec((c, GDN_W), lambda b, i: (b * nc + i, 3)),
            pl.BlockSpec((c, LANE), lambda b, i: (b * nc + i, 0)),
            pl.BlockSpec((8, 3 * GDN_W), lambda b, i: (0, 0)),
            vec, vec, vec, vec,
        ],
        out_specs=[
            pl.BlockSpec((c, GDN_W), lambda b, i: (b * nc + i, 0)),
            pl.BlockSpec((c, LANE), lambda b, i: (b * nc + i, 0)),
            pl.BlockSpec((1, 8, c), lambda b, i: (b, 0, i)),
        ],
        out_shape=[
            jax.ShapeDtypeStruct((m, GDN_W), BF16),
            jax.ShapeDtypeStruct((m, LANE), F32),
            jax.ShapeDtypeStruct((batch, 8, seq), F32),
        ],
        scratch_shapes=[
            pltpu.VMEM((GDN_HEADS, GDN_HEAD_DIM, GDN_HEAD_DIM), F32),
            pltpu.VMEM((1, LANE), F32),
            pltpu.VMEM((8 + c, 3 * GDN_W), F32),
        ],
        compiler_params=_cparams(("parallel", "arbitrary")),
        name="gated_deltanet",
    )(gdn_in, gdn_in, gdn_in, small, cw, alog, dt, fb, gw)


def _outproj_ffn_kernel(x_ref, ya_ref, yb_ref, woa_ref, wob_ref, nw_ref, wg_ref, wu_ref, wd_ref, fw_ref, o_ref,
                        *, final_norm):
    x1 = x_ref[...] + _dot(ya_ref[...], woa_ref[...]) + _dot(yb_ref[...], wob_ref[...])
    hb = _rms(x1, nw_ref[...]).astype(BF16)
    g = _dot(hb, wg_ref[...])
    u = _dot(hb, wu_ref[...])
    a = (_silu(g) * u).astype(BF16)
    out = x1 + _dot(a, wd_ref[...])
    if final_norm:
        out = _rms(out, fw_ref[...])
    o_ref[...] = out


def _outproj_ffn(x, ya, yb, woa, wob, nw, wg, wu, wd, fw, final_norm):
    m = x.shape[0]
    tm = min(TM_FFN, m)
    half = ya.shape[1]

    def resident(shape):
        return pl.BlockSpec(shape, lambda i: (0, 0), pipeline_mode=pl.Buffered(1))

    return pl.pallas_call(
        functools.partial(_outproj_ffn_kernel, final_norm=final_norm),
        grid=(m // tm,),
        in_specs=[
            pl.BlockSpec((tm, D_MODEL), lambda i: (i, 0)),
            pl.BlockSpec((tm, half), lambda i: (i, 0)),
            pl.BlockSpec((tm, half), lambda i: (i, 0)),
            resident((half, D_MODEL)),
            resident((half, D_MODEL)),
            pl.BlockSpec((1, D_MODEL), lambda i: (0, 0)),
            resident((D_MODEL, D_FF)),
            resident((D_MODEL, D_FF)),
            resident((D_FF, D_MODEL)),
            pl.BlockSpec((1, D_MODEL), lambda i: (0, 0)),
        ],
        out_specs=pl.BlockSpec((tm, D_MODEL), lambda i: (i, 0)),
        out_shape=jax.ShapeDtypeStruct((m, D_MODEL), F32),
        compiler_params=_cparams(("parallel",)),
        name="outproj_ffn",
    )(x, ya, yb, woa, wob, nw, wg, wu, wd, fw)


def _rotary_tables(seq):
    half = ROT_DIM // 2
    inv_freq = 1.0 / (ROPE_THETA ** (jnp.arange(half, dtype=F32) / half))
    ang = jnp.arange(seq, dtype=F32)[:, None] * inv_freq[None, :]
    lane = jnp.arange(LANE)
    within = lane % DIFF_HEAD_DIM
    idx = within % half
    cos_l = jnp.cos(ang)[:, idx]
    sin_l = jnp.sin(ang)[:, idx]
    rot_c = jnp.where(within < ROT_DIM, cos_l, 1.0)
    rot_sa = jnp.where(within < half, -sin_l, 0.0)
    rot_sb = jnp.where((within >= half) & (within < ROT_DIM), sin_l, 0.0)
    return rot_c, rot_sa, rot_sb


def _lane_row(vals, offset):
    return jnp.zeros((1, LANE), F32).at[0, offset:offset + vals.shape[0]].set(vals.astype(F32))


def kernel(x, mix_norm_w, ffn_norm_w, final_norm_w, w_in_even, conv_w, conv_b, conv_ln_w, conv_ln_b, lambda_q1, lambda_k1, lambda_q2, lambda_k2, diff_subln_w, gdn_conv_w, w_in_odd, gdn_a_log, gdn_dt_bias, gdn_norm_w, fox_forget_bias, w_out, w_gate, w_up, w_down):
    batch, seq, d = x.shape
    m = batch * seq
    xf = x.reshape(m, d).astype(F32)
    rot_c, rot_sa, rot_sb = _rotary_tables(seq)
    row = lambda v: v.reshape(1, -1).astype(F32)

    for l in range(DEPTH):
        i = l // 2
        nw = row(mix_norm_w[l])
        if l % 2 == 0:
            lambda_init = 0.8 - 0.6 * math.exp(-0.3 * l)
            conv_in, qkv = _even_inproj(xf, nw, w_in_even[i].astype(BF16), rot_c, rot_sa, rot_sb, seq)
            cw = jnp.zeros((CONV_HALO, CONV_CH), F32).at[:CONV_WIDTH].set(conv_w[i].astype(F32))
            ya = _conv_module(conv_in, cw, row(conv_b[i]), row(conv_ln_w[i]), row(conv_ln_b[i]), batch, seq)
            yb = _diff_attention(qkv, row(lambda_q1[i]), row(lambda_k1[i]), row(lambda_q2[i]), row(lambda_k2[i]),
                                 row(diff_subln_w[i]), lambda_init, batch, seq)
        else:
            w = w_in_odd[i]
            o_fd = 4 * GDN_W + 2 * GDN_HEADS + 3 * FOX_W
            w_small = jnp.zeros((d, LANE), w.dtype)
            w_small = w_small.at[:, SM_F:SM_F + 4].set(w[:, o_fd:o_fd + 4])
            w_small = w_small.at[:, SM_B:SM_B + 4].set(w[:, 4 * GDN_W:4 * GDN_W + 4])
            w_small = w_small.at[:, SM_A:SM_A + 4].set(w[:, 4 * GDN_W + 4:4 * GDN_W + 8])
            w_re = jnp.concatenate([w[:, :4 * GDN_W], w[:, 4 * GDN_W + 8:o_fd], w_small], axis=1).astype(BF16)
            gdn_in, small, fox = _odd_inproj(xf, nw, w_re, seq)
            cw = jnp.zeros((8, 3 * GDN_W), F32).at[:GDN_CONV].set(gdn_conv_w[i].astype(F32))
            ya, c_col, c_row = _gdn(gdn_in, small, cw, _lane_row(gdn_a_log[i], SM_A), _lane_row(gdn_dt_bias[i], SM_A),
                                    _lane_row(fox_forget_bias[i], SM_F), row(gdn_norm_w[i]), batch, seq)
            yb = _fox_attention(fox, c_col, c_row, batch, seq)
        wo = w_out[l].astype(BF16)
        half = wo.shape[0] // 2
        xf = _outproj_ffn(xf, ya, yb, wo[:half], wo[half:], row(ffn_norm_w[l]), w_gate[l].astype(BF16),
                          w_up[l].astype(BF16), w_down[l].astype(BF16), row(final_norm_w), l == DEPTH - 1)
    return xf.reshape(batch, seq, d).astype(x.dtype)
```

```python
import functools
import math

import jax
import jax.numpy as jnp
from jax import lax
from jax.experimental import pallas as pl
from jax.experimental.pallas import tpu as pltpu

F32 = jnp.float32
BF16 = jnp.bfloat16
HIGHEST = lax.Precision.HIGHEST

D_MODEL = 1024
DEPTH = 4
CONV_CH = 512
CONV_WIDTH = 31
DIFF_HEADS = 4
DIFF_HEAD_DIM = 64
ROPE_THETA = 500000.0
ROT_DIM = 16
GDN_HEADS = 4
GDN_HEAD_DIM = 128
GDN_W = 512
GDN_CONV = 4
FOX_HEADS = 4
FOX_W = 512
D_FF = 2816
LANE = 128
GDN_CHUNK = 128
NEG_BIG = -1e30

TM_PROJ = 512
TM_FFN = 512
T_CONV = 512
CONV_HALO = 32
CONV_ROWS = 32
TQ_DIFF = 256
TQ_FOX = 512
TK_ATTN = 512
LOG2E = math.log2(math.e)
VMEM_LIMIT = 56 * 1024 * 1024


def _cparams(sem):
    return pltpu.CompilerParams(dimension_semantics=sem, vmem_limit_bytes=VMEM_LIMIT)


def _rms(x, w, eps=1e-6):
    return x * lax.rsqrt(jnp.mean(x * x, axis=-1, keepdims=True) + eps) * w


def _sigmoid(x):
    return 1.0 / (1.0 + jnp.exp(-x))


def _silu(x):
    return x * _sigmoid(x)


def _softplus(x):
    return jnp.maximum(x, 0.0) + jnp.log(1.0 + jnp.exp(-jnp.abs(x)))


def _dot(a, b):
    return jnp.dot(a, b, preferred_element_type=F32)


def _dot_nt(a, b):
    return lax.dot_general(a, b, (((1,), (1,)), ((), ())), preferred_element_type=F32)


def _dot_hi(a, b):
    return jnp.dot(a, b, preferred_element_type=F32, precision=HIGHEST)


def _even_inproj_kernel(x_ref, nw_ref, w_ref, c_ref, sa_ref, sb_ref, conv_ref, qkv_ref):
    hb = _rms(x_ref[...], nw_ref[...]).astype(BF16)
    conv_ref[...] = _dot(hb, w_ref[:, :2 * CONV_CH])
    qk = _dot(hb, w_ref[:, 2 * CONV_CH:2 * CONV_CH + 1024])
    c = c_ref[...]
    sa = sa_ref[...]
    sb = sb_ref[...]
    for j in range(8):
        blk = qk[:, LANE * j:LANE * (j + 1)]
        rot = blk * c + pltpu.roll(blk, LANE - ROT_DIM // 2, 1) * sa + pltpu.roll(blk, ROT_DIM // 2, 1) * sb
        if j < 4:
            rot = rot * (DIFF_HEAD_DIM ** -0.5 * LOG2E)
        qkv_ref[:, LANE * j:LANE * (j + 1)] = rot.astype(BF16)
    qkv_ref[:, 1024:] = _dot(hb, w_ref[:, 2 * CONV_CH + 1024:]).astype(BF16)


def _even_inproj(x, nw, w, rot_c, rot_sa, rot_sb, seq):
    m = x.shape[0]
    tm = min(TM_PROJ, seq)
    nrot = seq // tm
    n_in = w.shape[1]
    return pl.pallas_call(
        _even_inproj_kernel,
        grid=(m // tm,),
        in_specs=[
            pl.BlockSpec((tm, D_MODEL), lambda i: (i, 0)),
            pl.BlockSpec((1, D_MODEL), lambda i: (0, 0)),
            pl.BlockSpec((D_MODEL, n_in), lambda i: (0, 0), pipeline_mode=pl.Buffered(1)),
            pl.BlockSpec((tm, LANE), lambda i: (i % nrot, 0)),
            pl.BlockSpec((tm, LANE), lambda i: (i % nrot, 0)),
            pl.BlockSpec((tm, LANE), lambda i: (i % nrot, 0)),
        ],
        out_specs=[
            pl.BlockSpec((tm, 2 * CONV_CH), lambda i: (i, 0)),
            pl.BlockSpec((tm, 1536), lambda i: (i, 0)),
        ],
        out_shape=[
            jax.ShapeDtypeStruct((m, 2 * CONV_CH), F32),
            jax.ShapeDtypeStruct((m, 1536), BF16),
        ],
        compiler_params=_cparams(("parallel",)),
        name="even_inproj",
    )(x, nw, w, rot_c, rot_sa, rot_sb)


def _odd_inproj_kernel(x_ref, nw_ref, w_ref, gdn_ref, small_ref, fox_ref):
    hb = _rms(x_ref[...], nw_ref[...]).astype(BF16)
    gdn_ref[...] = _dot(hb, w_ref[:, :2048])
    y = _dot(hb, w_ref[:, 2048:3584])
    fox_ref[:, :FOX_W] = (y[:, :FOX_W] * (GDN_HEAD_DIM ** -0.5 * LOG2E)).astype(BF16)
    fox_ref[:, FOX_W:] = y[:, FOX_W:].astype(BF16)
    small_ref[...] = _dot(hb, w_ref[:, 3584:])


def _odd_inproj(x, nw, w, seq):
    m = x.shape[0]
    tm = min(TM_PROJ, seq)
    n_in = w.shape[1]
    return pl.pallas_call(
        _odd_inproj_kernel,
        grid=(m // tm,),
        in_specs=[
            pl.BlockSpec((tm, D_MODEL), lambda i: (i, 0)),
            pl.BlockSpec((1, D_MODEL), lambda i: (0, 0)),
            pl.BlockSpec((D_MODEL, n_in), lambda i: (0, 0), pipeline_mode=pl.Buffered(1)),
        ],
        out_specs=[
            pl.BlockSpec((tm, 2048), lambda i: (i, 0)),
            pl.BlockSpec((tm, LANE), lambda i: (i, 0)),
            pl.BlockSpec((tm, 1536), lambda i: (i, 0)),
        ],
        out_shape=[
            jax.ShapeDtypeStruct((m, 2048), F32),
            jax.ShapeDtypeStruct((m, LANE), F32),
            jax.ShapeDtypeStruct((m, 1536), BF16),
        ],
        compiler_params=_cparams(("parallel",)),
        name="odd_inproj",
    )(x, nw, w)


def _conv_kernel(val_ref, gate_ref, pval_ref, pgate_ref, cw_ref, cb_ref, lw_ref, lb_ref, o_ref, buf_ref, *, t):
    i = pl.program_id(1)
    pu = pval_ref[...] * _sigmoid(pgate_ref[...])
    buf_ref[0:CONV_HALO, :] = jnp.where(i > 0, pu, 0.0)
    buf_ref[CONV_HALO:CONV_HALO + t, :] = val_ref[...] * _sigmoid(gate_ref[...])
    off = CONV_HALO - (CONV_WIDTH - 1)
    cb = cb_ref[...]
    lw = lw_ref[...]
    lb = lb_ref[...]
    for r in range(t // CONV_ROWS):
        r0 = r * CONV_ROWS
        acc = jnp.broadcast_to(cb, (CONV_ROWS, CONV_CH))
        for j in range(CONV_WIDTH):
            acc = acc + cw_ref[j:j + 1, :] * buf_ref[r0 + off + j:r0 + off + j + CONV_ROWS, :]
        mu = jnp.mean(acc, axis=-1, keepdims=True)
        xc = acc - mu
        var = jnp.mean(xc * xc, axis=-1, keepdims=True)
        y = xc * lax.rsqrt(var + 1e-5) * lw + lb
        o_ref[r0:r0 + CONV_ROWS, :] = _silu(y).astype(o_ref.dtype)


def _conv_module(conv_in, cw, cb, lw, lb, batch, seq):
    m = conv_in.shape[0]
    t = min(T_CONV, seq)
    nt = seq // t
    hb = t // CONV_HALO

    def prev_map(col):
        return lambda b, i: (jnp.maximum((b * nt + i) * hb - 1, 0), col)

    return pl.pallas_call(
        functools.partial(_conv_kernel, t=t),
        grid=(batch, nt),
        in_specs=[
            pl.BlockSpec((t, CONV_CH), lambda b, i: (b * nt + i, 0)),
            pl.BlockSpec((t, CONV_CH), lambda b, i: (b * nt + i, 1)),
            pl.BlockSpec((CONV_HALO, CONV_CH), prev_map(0)),
            pl.BlockSpec((CONV_HALO, CONV_CH), prev_map(1)),
            pl.BlockSpec((CONV_HALO, CONV_CH), lambda b, i: (0, 0)),
            pl.BlockSpec((1, CONV_CH), lambda b, i: (0, 0)),
            pl.BlockSpec((1, CONV_CH), lambda b, i: (0, 0)),
            pl.BlockSpec((1, CONV_CH), lambda b, i: (0, 0)),
        ],
        out_specs=pl.BlockSpec((t, CONV_CH), lambda b, i: (b * nt + i, 0)),
        out_shape=jax.ShapeDtypeStruct((m, CONV_CH), BF16),
        scratch_shapes=[pltpu.VMEM((CONV_HALO + t, CONV_CH), F32)],
        compiler_params=_cparams(("parallel", "parallel")),
        name="conv_module",
    )(conv_in, conv_in, conv_in, conv_in, cw, cb, lw, lb)


def _flash_scratch(rows, tk):
    return [pltpu.VMEM((rows, tk), F32), pltpu.VMEM((rows, tk), F32),
            pltpu.VMEM((rows, tk), BF16), pltpu.VMEM((rows, tk), BF16),
            pltpu.VMEM((rows, LANE), F32), pltpu.VMEM((rows, 2 * LANE), F32)]


def _flash(q_ref, k_ref, v_ref, scratch, qi, tq, tk, row_bias_ref, col_bias_fn):
    s_a, s_b, p_a, p_b, m_ref, acc_ref = scratch
    rows = q_ref.shape[0]
    nblk = tk // LANE
    n_full = (qi * tq) // tk
    ones = jnp.ones((tk, LANE), BF16)

    def blk(c):
        return slice(c * LANE, (c + 1) * LANE)

    def put_scores(s_ref, j):
        start = pl.multiple_of(j * tk, tk)
        s = _dot_nt(q_ref[...], k_ref[pl.ds(start, tk), :])
        if row_bias_ref is None:
            s_ref[...] = s
        else:
            cb = col_bias_fn(start)
            for c in range(nblk):
                s_ref[:, blk(c)] = s[:, blk(c)] + row_bias_ref[...] + cb[:, blk(c)]

    def pv(p_ref, j):
        v = v_ref[pl.ds(pl.multiple_of(j * tk, tk), tk), :]
        return _dot(p_ref[...], jnp.concatenate([v, ones], axis=1))

    def stage(s_cur, s_nxt, p_cur, p_prv, j, masked):
        if s_nxt is not None:
            put_scores(s_nxt, j + 1)
        pv_prev = pv(p_prv, jnp.maximum(j - 1, 0))

        def sblk(c):
            s = s_cur[:, blk(c)]
            if masked:
                rpos = lax.broadcasted_iota(jnp.int32, (rows, LANE), 0)
                if rows > tq:
                    rpos = jnp.where(rpos >= tq, rpos - tq, rpos)
                cpos = lax.broadcasted_iota(jnp.int32, (rows, LANE), 1) + (c * LANE + j * tk - qi * tq)
                s = jnp.where(cpos <= rpos, s, NEG_BIG)
            return s

        mx = sblk(0)
        for c in range(1, nblk):
            mx = jnp.maximum(mx, sblk(c))
        m_old = m_ref[...]
        m_new = jnp.maximum(m_old, jnp.broadcast_to(jnp.max(mx, axis=-1, keepdims=True), (rows, LANE)))
        alpha = jnp.exp2(m_old - m_new)
        m_ref[...] = m_new
        for c in range(nblk):
            p_cur[:, blk(c)] = jnp.exp2(sblk(c) - m_new).astype(BF16)
        for c in range(2):
            acc_ref[:, blk(c)] = (acc_ref[:, blk(c)] + pv_prev[:, blk(c)]) * alpha

    m_ref[...] = jnp.full(m_ref.shape, NEG_BIG, F32)
    acc_ref[...] = jnp.zeros(acc_ref.shape, F32)
    p_b[...] = jnp.zeros(p_b.shape, BF16)
    put_scores(s_a, 0)

    def pair(t, carry):
        stage(s_a, s_b, p_a, p_b, 2 * t, False)
        stage(s_b, s_a, p_b, p_a, 2 * t + 1, False)
        return carry

    lax.fori_loop(0, n_full // 2, pair, 0)

    @pl.when(n_full % 2 == 0)
    def _():
        stage(s_a, None, p_a, p_b, n_full, True)
        acc_ref[...] += pv(p_a, n_full)

    @pl.when(n_full % 2 == 1)
    def _():
        stage(s_a, s_b, p_a, p_b, n_full - 1, False)
        stage(s_b, None, p_b, p_a, n_full, True)
        acc_ref[...] += pv(p_b, n_full)


def _diff_attn_kernel(q_ref, k_ref, v_ref, lq1_ref, lk1_ref, lq2_ref, lk2_ref, sw_ref, o_ref, qs_ref, *scratch,
                      tq, tk, lambda_init):
    qi = pl.program_id(2)
    q = q_ref[...]
    lane = lax.broadcasted_iota(jnp.int32, q.shape, 1)
    zero = jnp.zeros_like(q)
    qs_ref[0:tq, :] = jnp.where(lane < DIFF_HEAD_DIM, q, zero)
    qs_ref[tq:2 * tq, :] = jnp.where(lane >= DIFF_HEAD_DIM, q, zero)
    _flash(qs_ref, k_ref, v_ref, scratch, qi, tq, tk, None, None)
    acc_ref = scratch[-1]
    lam = (jnp.exp(jnp.sum(lq1_ref[...] * lk1_ref[...], axis=-1, keepdims=True))
           - jnp.exp(jnp.sum(lq2_ref[...] * lk2_ref[...], axis=-1, keepdims=True)) + lambda_init)
    o = (acc_ref[0:tq, 0:LANE] / acc_ref[0:tq, LANE:2 * LANE]
         - lam * (acc_ref[tq:2 * tq, 0:LANE] / acc_ref[tq:2 * tq, LANE:2 * LANE]))
    o_ref[...] = (_rms(o, sw_ref[...]) * (1.0 - lambda_init)).astype(o_ref.dtype)


def _diff_attention(qkv, lq1, lk1, lq2, lk2, sw, lambda_init, batch, seq):
    m = qkv.shape[0]
    tq = min(TQ_DIFF, seq)
    tk = min(TK_ATTN, seq)
    assert tk % tq == 0
    nq = seq // tq
    vec = pl.BlockSpec((1, DIFF_HEAD_DIM), lambda b, h, i: (0, 0))
    return pl.pallas_call(
        functools.partial(_diff_attn_kernel, tq=tq, tk=tk, lambda_init=lambda_init),
        grid=(batch, DIFF_HEADS, nq),
        in_specs=[
            pl.BlockSpec((tq, LANE), lambda b, h, i: (b * nq + i, h)),
            pl.BlockSpec((seq, LANE), lambda b, h, i: (b, 4 + h)),
            pl.BlockSpec((seq, LANE), lambda b, h, i: (b, 8 + h)),
            vec, vec, vec, vec,
            pl.BlockSpec((1, LANE), lambda b, h, i: (0, 0)),
        ],
        out_specs=pl.BlockSpec((tq, LANE), lambda b, h, i: (b * nq + i, h)),
        out_shape=jax.ShapeDtypeStruct((m, 4 * LANE), BF16),
        scratch_shapes=[pltpu.VMEM((2 * tq, LANE), BF16)] + _flash_scratch(2 * tq, tk),
        compiler_params=_cparams(("parallel", "parallel", "parallel")),
        name="diff_attention",
    )(qkv, qkv, qkv, lq1, lk1, lq2, lk2, sw)


def _fox_attn_kernel(q_ref, k_ref, v_ref, cq_ref, ck_ref, o_ref, kbias_ref, qbias_ref, *scratch, tq, tk):
    h = pl.program_id(1)
    qi = pl.program_id(2)
    cq_blk = cq_ref[...]
    lane = lax.broadcasted_iota(jnp.int32, cq_blk.shape, 1)
    cq = jnp.sum(jnp.where(lane == h, cq_blk, 0.0), axis=-1, keepdims=True)
    ck_blk = ck_ref[0]
    sub = lax.broadcasted_iota(jnp.int32, ck_blk.shape, 0)
    ck = jnp.sum(jnp.where(sub == h, ck_blk, 0.0), axis=0, keepdims=True)
    c_base = cq[0:1, :]
    kbias_ref[...] = (c_base - ck) * LOG2E

    def col_bias(start):
        return kbias_ref[:, pl.ds(start, tk)]

    qbias_ref[...] = jnp.broadcast_to((cq - c_base) * LOG2E, (tq, LANE))
    _flash(q_ref, k_ref, v_ref, scratch, qi, tq, tk, qbias_ref, col_bias)
    acc_ref = scratch[-1]
    o_ref[...] = (acc_ref[:, 0:LANE] / acc_ref[:, LANE:2 * LANE]).astype(o_ref.dtype)


def _fox_attention(fox, c_col, c_row, batch, seq):
    m = fox.shape[0]
    tq = min(TQ_FOX, seq)
    tk = min(TK_ATTN, seq)
    assert tk % tq == 0
    nq = seq // tq
    return pl.pallas_call(
        functools.partial(_fox_attn_kernel, tq=tq, tk=tk),
        grid=(batch, FOX_HEADS, nq),
        in_specs=[
            pl.BlockSpec((tq, LANE), lambda b, h, i: (b * nq + i, h)),
            pl.BlockSpec((seq, LANE), lambda b, h, i: (b, 4 + h)),
            pl.BlockSpec((seq, LANE), lambda b, h, i: (b, 8 + h)),
            pl.BlockSpec((tq, LANE), lambda b, h, i: (b * nq + i, 0)),
            pl.BlockSpec((1, 8, seq), lambda b, h, i: (b, 0, 0)),
        ],
        out_specs=pl.BlockSpec((tq, LANE), lambda b, h, i: (b * nq + i, h)),
        out_shape=jax.ShapeDtypeStruct((m, FOX_W), BF16),
        scratch_shapes=[pltpu.VMEM((1, seq), F32), pltpu.VMEM((tq, LANE), F32)] + _flash_scratch(tq, tk),
        compiler_params=_cparams(("parallel", "parallel", "parallel")),
        name="fox_attention",
    )(fox, fox, fox, c_col, c_row)


SM_F, SM_B, SM_A = 0, 4, 8


def _unit_lower_inverse(a, row, col):
    eye = (row == col).astype(F32)
    shift = 4
    same = jnp.right_shift(row, shift) == jnp.right_shift(col, shift)
    n1 = jnp.where(same, a, 0.0)
    n2 = _dot_hi(n1, n1)
    n4 = _dot_hi(n2, n2)
    n8 = _dot_hi(n4, n4)
    x = eye - n1
    x = x + _dot_hi(x, n2)
    x = x + _dot_hi(x, n4)
    x = x + _dot_hi(x, n8)
    while (1 << shift) < a.shape[0]:
        shift += 1
        wider = jnp.right_shift(row, shift) == jnp.right_shift(col, shift)
        r = jnp.where(wider, jnp.where(same, 0.0, a), 0.0)
        x = x - _dot_hi(_dot_hi(x, r), x)
        same = wider
    return x


def _gdn_kernel(qkv_ref, prev_ref, z_ref, small_ref, cw_ref, alog_ref, dt_ref, fb_ref, gw_ref,
                o_ref, ccol_ref, crow_ref, state_ref, carry_ref, buf_ref):
    i = pl.program_id(1)
    c = GDN_CHUNK

    @pl.when(i == 0)
    def _():
        state_ref[...] = jnp.zeros_like(state_ref)
        carry_ref[...] = jnp.zeros_like(carry_ref)

    buf_ref[0:8, :] = jnp.where(i > 0, prev_ref[...], 0.0)
    buf_ref[8:8 + c, :] = qkv_ref[...]
    off = 8 - (GDN_CONV - 1)
    y = cw_ref[0:1, :] * buf_ref[off:off + c, :]
    for j in range(1, GDN_CONV):
        y = y + cw_ref[j:j + 1, :] * buf_ref[off + j:off + j + c, :]
    y = _silu(y)

    row = lax.broadcasted_iota(jnp.int32, (c, c), 0)
    col = lax.broadcasted_iota(jnp.int32, (c, c), 1)
    causal = row >= col
    strict = row > col
    ltri = causal.astype(F32)

    sm = small_ref[...]
    xf = sm + fb_ref[...]
    logf = jnp.minimum(xf, 0.0) - jnp.log(1.0 + jnp.exp(-jnp.abs(xf)))
    cl = _dot_hi(ltri, logf) + carry_ref[...]
    carry_ref[...] = cl[c - 1:c, :]
    ccol_ref[...] = cl
    crow_ref[0] = cl.T[0:8, :]

    g = -jnp.exp(alog_ref[...]) * _softplus(sm + dt_ref[...])
    gc = _dot_hi(ltri, g)
    gct = gc.T
    beta_all = _sigmoid(sm)
    gw = gw_ref[...]

    for h in range(GDN_HEADS):
        qh = y[:, h * LANE:(h + 1) * LANE]
        kh = y[:, GDN_W + h * LANE:GDN_W + (h + 1) * LANE]
        vh = y[:, 2 * GDN_W + h * LANE:2 * GDN_W + (h + 1) * LANE]
        qn = qh * (lax.rsqrt(jnp.sum(qh * qh, axis=-1, keepdims=True) + 1e-6) * (GDN_HEAD_DIM ** -0.5))
        kn = kh * lax.rsqrt(jnp.sum(kh * kh, axis=-1, keepdims=True) + 1e-6)
        gcol = gc[:, SM_A + h:SM_A + h + 1]
        grow = gct[SM_A + h:SM_A + h + 1, :]
        beta = beta_all[:, SM_B + h:SM_B + h + 1]
        decay = jnp.where(causal, jnp.exp(jnp.where(causal, gcol - grow, 0.0)), 0.0)
        kb = kn * beta
        knb = kn.astype(BF16)
        a_mat = jnp.where(strict, _dot_nt(kb.astype(BF16), knb) * decay, 0.0)
        t_inv = _unit_lower_inverse(a_mat, row, col).astype(BF16)
        eg = jnp.exp(gcol)
        uw = _dot(t_inv, jnp.concatenate([vh * beta, kb * eg], axis=-1).astype(BF16))
        u = uw[:, :LANE]
        w = uw[:, LANE:]
        qk = jnp.where(causal, _dot_nt(qn.astype(BF16), knb) * decay, 0.0)
        s = state_ref[h]
        sb = s.astype(BF16)
        v_new = u - _dot(w.astype(BF16), sb)
        vnb = v_new.astype(BF16)
        o = _dot((qn * eg).astype(BF16), sb) + _dot(qk.astype(BF16), vnb)
        g_last = gcol[c - 1:c, :]
        kd = kn * jnp.exp(g_last - gcol)
        state_ref[h] = s * jnp.exp(g_last) + _dot(kd.T.astype(BF16), vnb)
        zh = z_ref[:, h * LANE:(h + 1) * LANE]
        o_ref[:, h * LANE:(h + 1) * LANE] = (_rms(o, gw) * _silu(zh)).astype(o_ref.dtype)


def _gdn(gdn_in, small, cw, alog, dt, fb, gw, batch, seq):
    m = gdn_in.shape[0]
    c = GDN_CHUNK
    nc = seq // c
    vec = pl.BlockSpec((1, LANE), lambda b, i: (0, 0))
    return pl.pallas_call(
        _gdn_kernel,
        grid=(batch, nc),
        in_specs=[
            pl.BlockSpec((c, 3 * GDN_W), lambda b, i: (b * nc + i, 0)),
            pl.BlockSpec((8, 3 * GDN_W), lambda b, i: (jnp.maximum((b * nc + i) * (c // 8) - 1, 0), 0)),
            pl.BlockSpec((c, GDN_W), lambda b, i: (b * nc + i, 3)),
            pl.BlockSpec((c, LANE), lambda b, i: (b * nc + i, 0)),
            pl.BlockSpec((8, 3 * GDN_W), lambda b, i: (0, 0)),
            vec, vec, vec, vec,
        ],
        out_specs=[
            pl.BlockSpec((c, GDN_W), lambda b, i: (b * nc + i, 0)),
            pl.BlockSpec((c, LANE), lambda b, i: (b * nc + i, 0)),
            pl.BlockSpec((1, 8, c), lambda b, i: (b, 0, i)),
        ],
        out_shape=[
            jax.ShapeDtypeStruct((m, GDN_W), BF16),
            jax.ShapeDtypeStruct((m, LANE), F32),
            jax.ShapeDtypeStruct((batch, 8, seq), F32),
        ],
        scratch_shapes=[
            pltpu.VMEM((GDN_HEADS, GDN_HEAD_DIM, GDN_HEAD_DIM), F32),
            pltpu.VMEM((1, LANE), F32),
            pltpu.VMEM((8 + c, 3 * GDN_W), F32),
        ],
        compiler_params=_cparams(("parallel", "arbitrary")),
        name="gated_deltanet",
    )(gdn_in, gdn_in, gdn_in, small, cw, alog, dt, fb, gw)


def _outproj_ffn_kernel(x_ref, ya_ref, yb_ref, woa_ref, wob_ref, nw_ref, wg_ref, wu_ref, wd_ref, fw_ref, o_ref,
                        *, final_norm):
    x1 = x_ref[...] + _dot(ya_ref[...], woa_ref[...]) + _dot(yb_ref[...], wob_ref[...])
    hb = _rms(x1, nw_ref[...]).astype(BF16)
    g = _dot(hb, wg_ref[...])
    u = _dot(hb, wu_ref[...])
    a = (_silu(g) * u).astype(BF16)
    out = x1 + _dot(a, wd_ref[...])
    if final_norm:
        out = _rms(out, fw_ref[...])
    o_ref[...] = out


def _outproj_ffn(x, ya, yb, woa, wob, nw, wg, wu, wd, fw, final_norm):
    m = x.shape[0]
    tm = min(TM_FFN, m)
    half = ya.shape[1]

    def resident(shape):
        return pl.BlockSpec(shape, lambda i: (0, 0), pipeline_mode=pl.Buffered(1))

    return pl.pallas_call(
        functools.partial(_outproj_ffn_kernel, final_norm=final_norm),
        grid=(m // tm,),
        in_specs=[
            pl.BlockSpec((tm, D_MODEL), lambda i: (i, 0)),
            pl.BlockSpec((tm, half), lambda i: (i, 0)),
            pl.BlockSpec((tm, half), lambda i: (i, 0)),
            resident((half, D_MODEL)),
            resident((half, D_MODEL)),
            pl.BlockSpec((1, D_MODEL), lambda i: (0, 0)),
            resident((D_MODEL, D_FF)),
            resident((D_MODEL, D_FF)),
            resident((D_FF, D_MODEL)),
            pl.BlockSpec((1, D_MODEL), lambda i: (0, 0)),
        ],
        out_specs=pl.BlockSpec((tm, D_MODEL), lambda i: (i, 0)),
        out_shape=jax.ShapeDtypeStruct((m, D_MODEL), F32),
        compiler_params=_cparams(("parallel",)),
        name="outproj_ffn",
    )(x, ya, yb, woa, wob, nw, wg, wu, wd, fw)


def _rotary_tables(seq):
    half = ROT_DIM // 2
    inv_freq = 1.0 / (ROPE_THETA ** (jnp.arange(half, dtype=F32) / half))
    ang = jnp.arange(seq, dtype=F32)[:, None] * inv_freq[None, :]
    lane = jnp.arange(LANE)
    within = lane % DIFF_HEAD_DIM
    idx = within % half
    cos_l = jnp.cos(ang)[:, idx]
    sin_l = jnp.sin(ang)[:, idx]
    rot_c = jnp.where(within < ROT_DIM, cos_l, 1.0)
    rot_sa = jnp.where(within < half, -sin_l, 0.0)
    rot_sb = jnp.where((within >= half) & (within < ROT_DIM), sin_l, 0.0)
    return rot_c, rot_sa, rot_sb


def _lane_row(vals, offset):
    return jnp.zeros((1, LANE), F32).at[0, offset:offset + vals.shape[0]].set(vals.astype(F32))


def kernel(x, mix_norm_w, ffn_norm_w, final_norm_w, w_in_even, conv_w, conv_b, conv_ln_w, conv_ln_b, lambda_q1, lambda_k1, lambda_q2, lambda_k2, diff_subln_w, gdn_conv_w, w_in_odd, gdn_a_log, gdn_dt_bias, gdn_norm_w, fox_forget_bias, w_out, w_gate, w_up, w_down):
    batch, seq, d = x.shape
    m = batch * seq
    xf = x.reshape(m, d).astype(F32)
    rot_c, rot_sa, rot_sb = _rotary_tables(seq)
    row = lambda v: v.reshape(1, -1).astype(F32)

    for l in range(DEPTH):
        i = l // 2
        nw = row(mix_norm_w[l])
        if l % 2 == 0:
            lambda_init = 0.8 - 0.6 * math.exp(-0.3 * l)
            conv_in, qkv = _even_inproj(xf, nw, w_in_even[i].astype(BF16), rot_c, rot_sa, rot_sb, seq)
            cw = jnp.zeros((CONV_HALO, CONV_CH), F32).at[:CONV_WIDTH].set(conv_w[i].astype(F32))
            ya = _conv_module(conv_in, cw, row(conv_b[i]), row(conv_ln_w[i]), row(conv_ln_b[i]), batch, seq)
            yb = _diff_attention(qkv, row(lambda_q1[i]), row(lambda_k1[i]), row(lambda_q2[i]), row(lambda_k2[i]),
                                 row(diff_subln_w[i]), lambda_init, batch, seq)
        else:
            w = w_in_odd[i]
            o_fd = 4 * GDN_W + 2 * GDN_HEADS + 3 * FOX_W
            w_small = jnp.zeros((d, LANE), w.dtype)
            w_small = w_small.at[:, SM_F:SM_F + 4].set(w[:, o_fd:o_fd + 4])
            w_small = w_small.at[:, SM_B:SM_B + 4].set(w[:, 4 * GDN_W:4 * GDN_W + 4])
            w_small = w_small.at[:, SM_A:SM_A + 4].set(w[:, 4 * GDN_W + 4:4 * GDN_W + 8])
            w_re = jnp.concatenate([w[:, :4 * GDN_W], w[:, 4 * GDN_W + 8:o_fd], w_small], axis=1).astype(BF16)
            gdn_in, small, fox = _odd_inproj(xf, nw, w_re, seq)
            cw = jnp.zeros((8, 3 * GDN_W), F32).at[:GDN_CONV].set(gdn_conv_w[i].astype(F32))
            ya, c_col, c_row = _gdn(gdn_in, small, cw, _lane_row(gdn_a_log[i], SM_A), _lane_row(gdn_dt_bias[i], SM_A),
                                    _lane_row(fox_forget_bias[i], SM_F), row(gdn_norm_w[i]), batch, seq)
            yb = _fox_attention(fox, c_col, c_row, batch, seq)
        wo = w_out[l].astype(BF16)
        half = wo.shape[0] // 2
        xf = _outproj_ffn(xf, ya, yb, wo[:half], wo[half:], row(ffn_norm_w[l]), w_gate[l].astype(BF16),
                          w_up[l].astype(BF16), w_down[l].astype(BF16), row(final_norm_w), l == DEPTH - 1)
    return xf.reshape(batch, seq, d).astype(x.dtype)
```

```python
import functools
import math

import jax
import jax.numpy as jnp
from jax import lax
from jax.experimental import pallas as pl
from jax.experimental.pallas import tpu as pltpu

F32 = jnp.float32
BF16 = jnp.bfloat16

D_MODEL = 1024
DEPTH = 4
CONV_CH = 512
CONV_WIDTH = 31
DIFF_HEADS = 4
DIFF_HEAD_DIM = 64
ROPE_THETA = 500000.0
ROT_DIM = 16
GDN_HEADS = 4
GDN_HEAD_DIM = 128
GDN_W = 512
GDN_CONV = 4
FOX_HEADS = 4
FOX_W = 512
D_FF = 2816
LANE = 128
SUBLANE = 8
GDN_CHUNK = 128
GDN_CHUNKS_PER_STEP = 2
NEG_BIG = -1e30

TM_PROJ = 512
TM_FFN = 512
T_CONV = 512
CONV_HALO = 32
CONV_ROWS = 32
TQ_DIFF = 256
TQ_FOX = 512
TK_ATTN = 512
LOG2E = math.log2(math.e)
VMEM_LIMIT = 56 * 1024 * 1024


def _cparams(sem):
    return pltpu.CompilerParams(dimension_semantics=sem, vmem_limit_bytes=VMEM_LIMIT)


def _rms(x, w, eps=1e-6):
    return x * lax.rsqrt(jnp.mean(x * x, axis=-1, keepdims=True) + eps) * w


def _sigmoid(x):
    return 1.0 / (1.0 + jnp.exp(-x))


def _silu(x):
    return x * _sigmoid(x)


def _softplus(x):
    return jnp.maximum(x, 0.0) + jnp.log(1.0 + jnp.exp(-jnp.abs(x)))


def _dot(a, b):
    return jnp.dot(a, b, preferred_element_type=F32)


def _dot_nt(a, b):
    return lax.dot_general(a, b, (((1,), (1,)), ((), ())), preferred_element_type=F32)


def _split2(x):
    hi = x.astype(BF16)
    return hi, (x - hi.astype(F32)).astype(BF16)


def _dot3(a, b):
    a_hi, a_lo = a
    b_hi, b_lo = b
    return (_dot(jnp.concatenate([a_hi, a_lo], axis=1), jnp.concatenate([b_hi, b_hi], axis=0))
            + _dot(a_hi, b_lo))


def _cumsum_rows(ltri_b, x):
    hi = x.astype(BF16)
    r1 = x - hi.astype(F32)
    mid = r1.astype(BF16)
    lo = (r1 - mid.astype(F32)).astype(BF16)
    return _dot(jnp.concatenate([ltri_b, ltri_b, ltri_b], axis=1), jnp.concatenate([hi, mid, lo], axis=0))


def _even_inproj_kernel(x_ref, nw_ref, w_ref, c_ref, sa_ref, sb_ref, conv_ref, qkv_ref):
    hb = _rms(x_ref[...], nw_ref[...]).astype(BF16)
    conv_ref[...] = _dot(hb, w_ref[:, :2 * CONV_CH])
    qk = _dot(hb, w_ref[:, 2 * CONV_CH:2 * CONV_CH + 1024])
    c = c_ref[...]
    sa = sa_ref[...]
    sb = sb_ref[...]
    for j in range(8):
        blk = qk[:, LANE * j:LANE * (j + 1)]
        rot = blk * c + pltpu.roll(blk, LANE - ROT_DIM // 2, 1) * sa + pltpu.roll(blk, ROT_DIM // 2, 1) * sb
        if j < 4:
            rot = rot * (DIFF_HEAD_DIM ** -0.5 * LOG2E)
        qkv_ref[:, LANE * j:LANE * (j + 1)] = rot.astype(BF16)
    qkv_ref[:, 1024:] = _dot(hb, w_ref[:, 2 * CONV_CH + 1024:]).astype(BF16)


def _even_inproj(x, nw, w, rot_c, rot_sa, rot_sb, seq):
    m = x.shape[0]
    tm = min(TM_PROJ, seq)
    nrot = seq // tm
    n_in = w.shape[1]
    return pl.pallas_call(
        _even_inproj_kernel,
        grid=(m // tm,),
        in_specs=[
            pl.BlockSpec((tm, D_MODEL), lambda i: (i, 0)),
            pl.BlockSpec((1, D_MODEL), lambda i: (0, 0)),
            pl.BlockSpec((D_MODEL, n_in), lambda i: (0, 0), pipeline_mode=pl.Buffered(1)),
            pl.BlockSpec((tm, LANE), lambda i: (i % nrot, 0)),
            pl.BlockSpec((tm, LANE), lambda i: (i % nrot, 0)),
            pl.BlockSpec((tm, LANE), lambda i: (i % nrot, 0)),
        ],
        out_specs=[
            pl.BlockSpec((tm, 2 * CONV_CH), lambda i: (i, 0)),
            pl.BlockSpec((tm, 1536), lambda i: (i, 0)),
        ],
        out_shape=[
            jax.ShapeDtypeStruct((m, 2 * CONV_CH), F32),
            jax.ShapeDtypeStruct((m, 1536), BF16),
        ],
        compiler_params=_cparams(("parallel",)),
        name="even_inproj",
    )(x, nw, w, rot_c, rot_sa, rot_sb)


def _odd_inproj_kernel(x_ref, nw_ref, w_ref, gdn_ref, small_ref, fox_ref):
    hb = _rms(x_ref[...], nw_ref[...]).astype(BF16)
    gdn_ref[...] = _dot(hb, w_ref[:, :2048])
    y = _dot(hb, w_ref[:, 2048:3584])
    fox_ref[:, :FOX_W] = (y[:, :FOX_W] * (GDN_HEAD_DIM ** -0.5 * LOG2E)).astype(BF16)
    fox_ref[:, FOX_W:] = y[:, FOX_W:].astype(BF16)
    small_ref[...] = _dot(hb, w_ref[:, 3584:])


def _odd_inproj(x, nw, w, seq):
    m = x.shape[0]
    tm = min(TM_PROJ, seq)
    n_in = w.shape[1]
    return pl.pallas_call(
        _odd_inproj_kernel,
        grid=(m // tm,),
        in_specs=[
            pl.BlockSpec((tm, D_MODEL), lambda i: (i, 0)),
            pl.BlockSpec((1, D_MODEL), lambda i: (0, 0)),
            pl.BlockSpec((D_MODEL, n_in), lambda i: (0, 0), pipeline_mode=pl.Buffered(1)),
        ],
        out_specs=[
            pl.BlockSpec((tm, 2048), lambda i: (i, 0)),
            pl.BlockSpec((tm, LANE), lambda i: (i, 0)),
            pl.BlockSpec((tm, 1536), lambda i: (i, 0)),
        ],
        out_shape=[
            jax.ShapeDtypeStruct((m, 2048), F32),
            jax.ShapeDtypeStruct((m, LANE), F32),
            jax.ShapeDtypeStruct((m, 1536), BF16),
        ],
        compiler_params=_cparams(("parallel",)),
        name="odd_inproj",
    )(x, nw, w)


def _conv_kernel(val_ref, gate_ref, pval_ref, pgate_ref, cw_ref, cb_ref, lw_ref, lb_ref, o_ref, buf_ref, *, t):
    i = pl.program_id(1)
    pu = pval_ref[...] * _sigmoid(pgate_ref[...])
    buf_ref[0, 0:CONV_HALO, :] = jnp.where(i > 0, pu, 0.0)
    buf_ref[0, CONV_HALO:CONV_HALO + t, :] = val_ref[...] * _sigmoid(gate_ref[...])
    span = t + CONV_HALO - SUBLANE
    for s in range(1, SUBLANE):
        buf_ref[s, 0:span, :] = buf_ref[0, s:s + span, :]
    off = CONV_HALO - (CONV_WIDTH - 1)
    cb = cb_ref[...]
    lw = lw_ref[...]
    lb = lb_ref[...]
    for r in range(t // CONV_ROWS):
        r0 = r * CONV_ROWS
        acc = jnp.broadcast_to(cb, (CONV_ROWS, CONV_CH))
        for j in range(CONV_WIDTH):
            a, s = divmod(off + j, SUBLANE)
            acc = acc + cw_ref[j:j + 1, :] * buf_ref[s, r0 + a * SUBLANE:r0 + a * SUBLANE + CONV_ROWS, :]
        mu = jnp.mean(acc, axis=-1, keepdims=True)
        xc = acc - mu
        var = jnp.mean(xc * xc, axis=-1, keepdims=True)
        y = xc * lax.rsqrt(var + 1e-5) * lw + lb
        o_ref[r0:r0 + CONV_ROWS, :] = _silu(y).astype(o_ref.dtype)


def _conv_module(conv_in, cw, cb, lw, lb, batch, seq):
    m = conv_in.shape[0]
    t = min(T_CONV, seq)
    nt = seq // t
    hb = t // CONV_HALO

    def prev_map(col):
        return lambda b, i: (jnp.maximum((b * nt + i) * hb - 1, 0), col)

    return pl.pallas_call(
        functools.partial(_conv_kernel, t=t),
        grid=(batch, nt),
        in_specs=[
            pl.BlockSpec((t, CONV_CH), lambda b, i: (b * nt + i, 0)),
            pl.BlockSpec((t, CONV_CH), lambda b, i: (b * nt + i, 1)),
            pl.BlockSpec((CONV_HALO, CONV_CH), prev_map(0)),
            pl.BlockSpec((CONV_HALO, CONV_CH), prev_map(1)),
            pl.BlockSpec((CONV_HALO, CONV_CH), lambda b, i: (0, 0)),
            pl.BlockSpec((1, CONV_CH), lambda b, i: (0, 0)),
            pl.BlockSpec((1, CONV_CH), lambda b, i: (0, 0)),
            pl.BlockSpec((1, CONV_CH), lambda b, i: (0, 0)),
        ],
        out_specs=pl.BlockSpec((t, CONV_CH), lambda b, i: (b * nt + i, 0)),
        out_shape=jax.ShapeDtypeStruct((m, CONV_CH), BF16),
        scratch_shapes=[pltpu.VMEM((SUBLANE, CONV_HALO + t, CONV_CH), F32)],
        compiler_params=_cparams(("parallel", "parallel")),
        name="conv_module",
    )(conv_in, conv_in, conv_in, conv_in, cw, cb, lw, lb)


def _flash_scratch(rows, tk):
    return [pltpu.VMEM((rows, tk), F32), pltpu.VMEM((rows, tk), F32),
            pltpu.VMEM((rows, tk), BF16), pltpu.VMEM((rows, tk), BF16),
            pltpu.VMEM((rows, LANE), F32), pltpu.VMEM((rows, 2 * LANE), F32)]


def _flash(q_ref, k_ref, v_ref, scratch, qi, tq, tk, row_bias_ref, col_bias_fn):
    s_a, s_b, p_a, p_b, m_ref, acc_ref = scratch
    rows = q_ref.shape[0]
    nblk = tk // LANE
    n_full = (qi * tq) // tk
    ones = jnp.ones((tk, LANE), BF16)

    def blk(c):
        return slice(c * LANE, (c + 1) * LANE)

    def put_scores(s_ref, j):
        start = pl.multiple_of(j * tk, tk)
        s = _dot_nt(q_ref[...], k_ref[pl.ds(start, tk), :])
        if row_bias_ref is None:
            s_ref[...] = s
        else:
            cb = col_bias_fn(start)
            for c in range(nblk):
                s_ref[:, blk(c)] = s[:, blk(c)] + row_bias_ref[...] + cb[:, blk(c)]

    def pv(p_ref, j):
        v = v_ref[pl.ds(pl.multiple_of(j * tk, tk), tk), :]
        return _dot(p_ref[...], jnp.concatenate([v, ones], axis=1))

    def stage(s_cur, s_nxt, p_cur, p_prv, j, masked):
        if s_nxt is not None:
            put_scores(s_nxt, j + 1)
        pv_prev = pv(p_prv, jnp.maximum(j - 1, 0))

        def sblk(c):
            s = s_cur[:, blk(c)]
            if masked:
                rpos = lax.broadcasted_iota(jnp.int32, (rows, LANE), 0)
                if rows > tq:
                    rpos = jnp.where(rpos >= tq, rpos - tq, rpos)
                cpos = lax.broadcasted_iota(jnp.int32, (rows, LANE), 1) + (c * LANE + j * tk - qi * tq)
                s = jnp.where(cpos <= rpos, s, NEG_BIG)
            return s

        mx = sblk(0)
        for c in range(1, nblk):
            mx = jnp.maximum(mx, sblk(c))
        m_old = m_ref[...]
        m_new = jnp.maximum(m_old, jnp.broadcast_to(jnp.max(mx, axis=-1, keepdims=True), (rows, LANE)))
        alpha = jnp.exp2(m_old - m_new)
        m_ref[...] = m_new
        for c in range(nblk):
            p_cur[:, blk(c)] = jnp.exp2(sblk(c) - m_new).astype(BF16)
        for c in range(2):
            acc_ref[:, blk(c)] = (acc_ref[:, blk(c)] + pv_prev[:, blk(c)]) * alpha

    m_ref[...] = jnp.full(m_ref.shape, NEG_BIG, F32)
    acc_ref[...] = jnp.zeros(acc_ref.shape, F32)
    p_b[...] = jnp.zeros(p_b.shape, BF16)
    put_scores(s_a, 0)

    def pair(t, carry):
        stage(s_a, s_b, p_a, p_b, 2 * t, False)
        stage(s_b, s_a, p_b, p_a, 2 * t + 1, False)
        return carry

    lax.fori_loop(0, n_full // 2, pair, 0)

    @pl.when(n_full % 2 == 0)
    def _():
        stage(s_a, None, p_a, p_b, n_full, True)
        acc_ref[...] += pv(p_a, n_full)

    @pl.when(n_full % 2 == 1)
    def _():
        stage(s_a, s_b, p_a, p_b, n_full - 1, False)
        stage(s_b, None, p_b, p_a, n_full, True)
        acc_ref[...] += pv(p_b, n_full)


def _diff_attn_kernel(q_ref, k_ref, v_ref, lq1_ref, lk1_ref, lq2_ref, lk2_ref, sw_ref, o_ref, qs_ref, *scratch,
                      tq, tk, lambda_init):
    qi = pl.program_id(2)
    q = q_ref[...]
    lane = lax.broadcasted_iota(jnp.int32, q.shape, 1)
    zero = jnp.zeros_like(q)
    qs_ref[0:tq, :] = jnp.where(lane < DIFF_HEAD_DIM, q, zero)
    qs_ref[tq:2 * tq, :] = jnp.where(lane >= DIFF_HEAD_DIM, q, zero)
    _flash(qs_ref, k_ref, v_ref, scratch, qi, tq, tk, None, None)
    acc_ref = scratch[-1]
    lam = (jnp.exp(jnp.sum(lq1_ref[...] * lk1_ref[...], axis=-1, keepdims=True))
           - jnp.exp(jnp.sum(lq2_ref[...] * lk2_ref[...], axis=-1, keepdims=True)) + lambda_init)
    o = (acc_ref[0:tq, 0:LANE] / acc_ref[0:tq, LANE:2 * LANE]
         - lam * (acc_ref[tq:2 * tq, 0:LANE] / acc_ref[tq:2 * tq, LANE:2 * LANE]))
    o_ref[...] = (_rms(o, sw_ref[...]) * (1.0 - lambda_init)).astype(o_ref.dtype)


def _diff_attention(qkv, lq1, lk1, lq2, lk2, sw, lambda_init, batch, seq):
    m = qkv.shape[0]
    tq = min(TQ_DIFF, seq)
    tk = min(TK_ATTN, seq)
    assert tk % tq == 0
    nq = seq // tq
    vec = pl.BlockSpec((1, DIFF_HEAD_DIM), lambda b, h, i: (0, 0))
    return pl.pallas_call(
        functools.partial(_diff_attn_kernel, tq=tq, tk=tk, lambda_init=lambda_init),
        grid=(batch, DIFF_HEADS, nq),
        in_specs=[
            pl.BlockSpec((tq, LANE), lambda b, h, i: (b * nq + i, h)),
            pl.BlockSpec((seq, LANE), lambda b, h, i: (b, 4 + h)),
            pl.BlockSpec((seq, LANE), lambda b, h, i: (b, 8 + h)),
            vec, vec, vec, vec,
            pl.BlockSpec((1, LANE), lambda b, h, i: (0, 0)),
        ],
        out_specs=pl.BlockSpec((tq, LANE), lambda b, h, i: (b * nq + i, h)),
        out_shape=jax.ShapeDtypeStruct((m, 4 * LANE), BF16),
        scratch_shapes=[pltpu.VMEM((2 * tq, LANE), BF16)] + _flash_scratch(2 * tq, tk),
        compiler_params=_cparams(("parallel", "parallel", "parallel")),
        name="diff_attention",
    )(qkv, qkv, qkv, lq1, lk1, lq2, lk2, sw)


def _fox_attn_kernel(q_ref, k_ref, v_ref, cq_ref, ck_ref, o_ref, kbias_ref, qbias_ref, *scratch, tq, tk):
    h = pl.program_id(1)
    qi = pl.program_id(2)
    cq_blk = cq_ref[...]
    lane = lax.broadcasted_iota(jnp.int32, cq_blk.shape, 1)
    cq = jnp.sum(jnp.where(lane == h, cq_blk, 0.0), axis=-1, keepdims=True)
    ck_blk = ck_ref[0]
    sub = lax.broadcasted_iota(jnp.int32, ck_blk.shape, 0)
    ck = jnp.sum(jnp.where(sub == h, ck_blk, 0.0), axis=0, keepdims=True)
    c_base = cq[0:1, :]
    kbias_ref[...] = (c_base - ck) * LOG2E

    def col_bias(start):
        return kbias_ref[:, pl.ds(start, tk)]

    qbias_ref[...] = jnp.broadcast_to((cq - c_base) * LOG2E, (tq, LANE))
    _flash(q_ref, k_ref, v_ref, scratch, qi, tq, tk, qbias_ref, col_bias)
    acc_ref = scratch[-1]
    o_ref[...] = (acc_ref[:, 0:LANE] / acc_ref[:, LANE:2 * LANE]).astype(o_ref.dtype)


def _fox_attention(fox, c_col, c_row, batch, seq):
    m = fox.shape[0]
    tq = min(TQ_FOX, seq)
    tk = min(TK_ATTN, seq)
    assert tk % tq == 0
    nq = seq // tq
    return pl.pallas_call(
        functools.partial(_fox_attn_kernel, tq=tq, tk=tk),
        grid=(batch, FOX_HEADS, nq),
        in_specs=[
            pl.BlockSpec((tq, LANE), lambda b, h, i: (b * nq + i, h)),
            pl.BlockSpec((seq, LANE), lambda b, h, i: (b, 4 + h)),
            pl.BlockSpec((seq, LANE), lambda b, h, i: (b, 8 + h)),
            pl.BlockSpec((tq, LANE), lambda b, h, i: (b * nq + i, 0)),
            pl.BlockSpec((1, 8, seq), lambda b, h, i: (b, 0, 0)),
        ],
        out_specs=pl.BlockSpec((tq, LANE), lambda b, h, i: (b * nq + i, h)),
        out_shape=jax.ShapeDtypeStruct((m, FOX_W), BF16),
        scratch_shapes=[pltpu.VMEM((1, seq), F32), pltpu.VMEM((tq, LANE), F32)] + _flash_scratch(tq, tk),
        compiler_params=_cparams(("parallel", "parallel", "parallel")),
        name="fox_attention",
    )(fox, fox, fox, c_col, c_row)


SM_F, SM_B, SM_A = 0, 4, 8


def _unit_lower_inverses(mats, row, col):
    size = mats[0].shape[0]
    eye = (row == col).astype(F32)
    shift = 4
    same = jnp.right_shift(row, shift) == jnp.right_shift(col, shift)
    n1 = [jnp.where(same, a, 0.0) for a in mats]
    xs = [eye - n for n in n1]
    ps = [_split2(n) for n in n1]
    for _ in range(shift - 1):
        ps = [_split2(_dot3(p, p)) for p in ps]
        xs = [x + _dot3(_split2(x), p) for x, p in zip(xs, ps)]
    while (1 << shift) < size:
        shift += 1
        wider = jnp.right_shift(row, shift) == jnp.right_shift(col, shift)
        rs = [_split2(jnp.where(wider, jnp.where(same, 0.0, a), 0.0)) for a in mats]
        xss = [_split2(x) for x in xs]
        ys = [_split2(_dot3(x2, r)) for x2, r in zip(xss, rs)]
        xs = [x - _dot3(y, x2) for x, y, x2 in zip(xs, ys, xss)]
        same = wider
    return xs


def _gdn_kernel(qkv_ref, prev_ref, z_ref, small_ref, cw_ref, alog_ref, dt_ref, fb_ref, gw_ref,
                o_ref, ccol_ref, crow_ref, state_ref, carry_ref, buf_ref, *, nch):
    i = pl.program_id(1)
    c = GDN_CHUNK
    t = nch * c

    @pl.when(i == 0)
    def _():
        state_ref[...] = jnp.zeros_like(state_ref)
        carry_ref[...] = jnp.zeros_like(carry_ref)

    buf_ref[0:8, :] = jnp.where(i > 0, prev_ref[...], 0.0)
    buf_ref[8:8 + t, :] = qkv_ref[...]
    off = 8 - (GDN_CONV - 1)
    y = cw_ref[0:1, :] * buf_ref[off:off + t, :]
    for j in range(1, GDN_CONV):
        y = y + cw_ref[j:j + 1, :] * buf_ref[off + j:off + j + t, :]
    y = _silu(y)

    row = lax.broadcasted_iota(jnp.int32, (c, c), 0)
    col = lax.broadcasted_iota(jnp.int32, (c, c), 1)
    causal = row >= col
    strict = row > col
    ltri_b = jnp.where(causal, 1.0, 0.0).astype(BF16)

    sm = small_ref[...]
    xf = sm + fb_ref[...]
    logf = jnp.minimum(xf, 0.0) - jnp.log(1.0 + jnp.exp(-jnp.abs(xf)))
    g = -jnp.exp(alog_ref[...]) * _softplus(sm + dt_ref[...])
    is_f = lax.broadcasted_iota(jnp.int32, (1, LANE), 1) < SM_B
    summand = jnp.where(is_f, logf, g)
    beta_all = _sigmoid(sm)
    gw = gw_ref[...]
    carry = carry_ref[...]
    gcs = []
    for ch in range(nch):
        gc = _cumsum_rows(ltri_b, summand[ch * c:(ch + 1) * c, :])
        cl = gc + carry
        carry = jnp.where(is_f, cl[c - 1:c, :], 0.0)
        ccol_ref[ch * c:(ch + 1) * c, :] = cl
        crow_ref[0, :, ch * c:(ch + 1) * c] = cl.T[0:8, :]
        gcs.append((gc, gc.T))
    carry_ref[...] = carry

    pairs = [(ch, h) for ch in range(nch) for h in range(GDN_HEADS)]
    pre = []
    for ch, h in pairs:
        rows = slice(ch * c, (ch + 1) * c)
        gc, gct = gcs[ch]
        qh = y[rows, h * LANE:(h + 1) * LANE]
        kh = y[rows, GDN_W + h * LANE:GDN_W + (h + 1) * LANE]
        vh = y[rows, 2 * GDN_W + h * LANE:2 * GDN_W + (h + 1) * LANE]
        qn = qh * (lax.rsqrt(jnp.sum(qh * qh, axis=-1, keepdims=True) + 1e-6) * (GDN_HEAD_DIM ** -0.5))
        kn = kh * lax.rsqrt(jnp.sum(kh * kh, axis=-1, keepdims=True) + 1e-6)
        gcol = gc[:, SM_A + h:SM_A + h + 1]
        grow = gct[SM_A + h:SM_A + h + 1, :]
        beta = beta_all[rows, SM_B + h:SM_B + h + 1]
        decay = jnp.where(causal, jnp.exp(jnp.where(causal, gcol - grow, 0.0)), 0.0)
        kb = kn * beta
        knb = kn.astype(BF16)
        a_mat = jnp.where(strict, _dot_nt(kb.astype(BF16), knb) * decay, 0.0)
        eg = jnp.exp(gcol)
        g_last = gcol[c - 1:c, :]
        pre.append(dict(
            a=a_mat,
            rhs=jnp.concatenate([vh * beta, kb * eg], axis=-1).astype(BF16),
            qk=jnp.where(causal, _dot_nt(qn.astype(BF16), knb) * decay, 0.0).astype(BF16),
            qeg=(qn * eg).astype(BF16),
            kdt=(kn * jnp.exp(g_last - gcol)).T.astype(BF16),
            sdec=jnp.exp(g_last)))
    t_invs = _unit_lower_inverses([p["a"] for p in pre], row, col)
    uws = [_dot(ti.astype(BF16), p["rhs"]) for ti, p in zip(t_invs, pre)]

    states = [state_ref[h] for h in range(GDN_HEADS)]
    for ch in range(nch):
        rows = slice(ch * c, (ch + 1) * c)
        idx = [ch * GDN_HEADS + h for h in range(GDN_HEADS)]
        sbs = [s.astype(BF16) for s in states]
        vnbs = [(uws[k][:, :LANE] - _dot(uws[k][:, LANE:].astype(BF16), sb)).astype(BF16)
                for k, sb in zip(idx, sbs)]
        outs = [_dot(pre[k]["qeg"], sb) + _dot(pre[k]["qk"], vnb) for k, sb, vnb in zip(idx, sbs, vnbs)]
        states = [s * pre[k]["sdec"] + _dot(pre[k]["kdt"], vnb) for k, s, vnb in zip(idx, states, vnbs)]
        for h, o in enumerate(outs):
            zh = z_ref[rows, h * LANE:(h + 1) * LANE]
            o_ref[rows, h * LANE:(h + 1) * LANE] = (_rms(o, gw) * _silu(zh)).astype(o_ref.dtype)
    for h in range(GDN_HEADS):
        state_ref[h] = states[h]


def _gdn(gdn_in, small, cw, alog, dt, fb, gw, batch, seq):
    m = gdn_in.shape[0]
    nch = min(GDN_CHUNKS_PER_STEP, seq // GDN_CHUNK)
    t = nch * GDN_CHUNK
    nt = seq // t
    vec = pl.BlockSpec((1, LANE), lambda b, i: (0, 0))
    return pl.pallas_call(
        functools.partial(_gdn_kernel, nch=nch),
        grid=(batch, nt),
        in_specs=[
            pl.BlockSpec((t, 3 * GDN_W), lambda b, i: (b * nt + i, 0)),
            pl.BlockSpec((8, 3 * GDN_W), lambda b, i: (jnp.maximum((b * nt + i) * (t // 8) - 1, 0), 0)),
            pl.BlockSpec((t, GDN_W), lambda b, i: (b * nt + i, 3)),
            pl.BlockSpec((t, LANE), lambda b, i: (b * nt + i, 0)),
            pl.BlockSpec((8, 3 * GDN_W), lambda b, i: (0, 0)),
            vec, vec, vec, vec,
        ],
        out_specs=[
            pl.BlockSpec((t, GDN_W), lambda b, i: (b * nt + i, 0)),
            pl.BlockSpec((t, LANE), lambda b, i: (b * nt + i, 0)),
            pl.BlockSpec((1, 8, t), lambda b, i: (b, 0, i)),
        ],
        out_shape=[
            jax.ShapeDtypeStruct((m, GDN_W), BF16),
            jax.ShapeDtypeStruct((m, LANE), F32),
            jax.ShapeDtypeStruct((batch, 8, seq), F32),
        ],
        scratch_shapes=[
            pltpu.VMEM((GDN_HEADS, GDN_HEAD_DIM, GDN_HEAD_DIM), F32),
            pltpu.VMEM((1, LANE), F32),
            pltpu.VMEM((8 + t, 3 * GDN_W), F32),
        ],
        compiler_params=_cparams(("parallel", "arbitrary")),
        name="gated_deltanet",
    )(gdn_in, gdn_in, gdn_in, small, cw, alog, dt, fb, gw)


def _outproj_ffn_kernel(x_ref, ya_ref, yb_ref, woa_ref, wob_ref, nw_ref, wg_ref, wu_ref, wd_ref, fw_ref, o_ref,
                        *, final_norm):
    x1 = x_ref[...] + _dot(ya_ref[...], woa_ref[...]) + _dot(yb_ref[...], wob_ref[...])
    hb = _rms(x1, nw_ref[...]).astype(BF16)
    g = _dot(hb, wg_ref[...])
    u = _dot(hb, wu_ref[...])
    a = (_silu(g) * u).astype(BF16)
    out = x1 + _dot(a, wd_ref[...])
    if final_norm:
        out = _rms(out, fw_ref[...])
    o_ref[...] = out


def _outproj_ffn(x, ya, yb, woa, wob, nw, wg, wu, wd, fw, final_norm):
    m = x.shape[0]
    tm = min(TM_FFN, m)
    half = ya.shape[1]

    def resident(shape):
        return pl.BlockSpec(shape, lambda i: (0, 0), pipeline_mode=pl.Buffered(1))

    return pl.pallas_call(
        functools.partial(_outproj_ffn_kernel, final_norm=final_norm),
        grid=(m // tm,),
        in_specs=[
            pl.BlockSpec((tm, D_MODEL), lambda i: (i, 0)),
            pl.BlockSpec((tm, half), lambda i: (i, 0)),
            pl.BlockSpec((tm, half), lambda i: (i, 0)),
            resident((half, D_MODEL)),
            resident((half, D_MODEL)),
            pl.BlockSpec((1, D_MODEL), lambda i: (0, 0)),
            resident((D_MODEL, D_FF)),
            resident((D_MODEL, D_FF)),
            resident((D_FF, D_MODEL)),
            pl.BlockSpec((1, D_MODEL), lambda i: (0, 0)),
        ],
        out_specs=pl.BlockSpec((tm, D_MODEL), lambda i: (i, 0)),
        out_shape=jax.ShapeDtypeStruct((m, D_MODEL), F32),
        compiler_params=_cparams(("parallel",)),
        name="outproj_ffn",
    )(x, ya, yb, woa, wob, nw, wg, wu, wd, fw)


def _rotary_tables(seq):
    half = ROT_DIM // 2
    inv_freq = 1.0 / (ROPE_THETA ** (jnp.arange(half, dtype=F32) / half))
    ang = jnp.arange(seq, dtype=F32)[:, None] * inv_freq[None, :]
    lane = jnp.arange(LANE)
    within = lane % DIFF_HEAD_DIM
    idx = within % half
    cos_l = jnp.cos(ang)[:, idx]
    sin_l = jnp.sin(ang)[:, idx]
    rot_c = jnp.where(within < ROT_DIM, cos_l, 1.0)
    rot_sa = jnp.where(within < half, -sin_l, 0.0)
    rot_sb = jnp.where((within >= half) & (within < ROT_DIM), sin_l, 0.0)
    return rot_c, rot_sa, rot_sb


def _lane_row(vals, offset):
    return jnp.zeros((1, LANE), F32).at[0, offset:offset + vals.shape[0]].set(vals.astype(F32))


def kernel(x, mix_norm_w, ffn_norm_w, final_norm_w, w_in_even, conv_w, conv_b, conv_ln_w, conv_ln_b, lambda_q1, lambda_k1, lambda_q2, lambda_k2, diff_subln_w, gdn_conv_w, w_in_odd, gdn_a_log, gdn_dt_bias, gdn_norm_w, fox_forget_bias, w_out, w_gate, w_up, w_down):
    batch, seq, d = x.shape
    m = batch * seq
    xf = x.reshape(m, d).astype(F32)
    rot_c, rot_sa, rot_sb = _rotary_tables(seq)
    row = lambda v: v.reshape(1, -1).astype(F32)

    for l in range(DEPTH):
        i = l // 2
        nw = row(mix_norm_w[l])
        if l % 2 == 0:
            lambda_init = 0.8 - 0.6 * math.exp(-0.3 * l)
            conv_in, qkv = _even_inproj(xf, nw, w_in_even[i].astype(BF16), rot_c, rot_sa, rot_sb, seq)
            cw = jnp.zeros((CONV_HALO, CONV_CH), F32).at[:CONV_WIDTH].set(conv_w[i].astype(F32))
            ya = _conv_module(conv_in, cw, row(conv_b[i]), row(conv_ln_w[i]), row(conv_ln_b[i]), batch, seq)
            yb = _diff_attention(qkv, row(lambda_q1[i]), row(lambda_k1[i]), row(lambda_q2[i]), row(lambda_k2[i]),
                                 row(diff_subln_w[i]), lambda_init, batch, seq)
        else:
            w = w_in_odd[i]
            o_fd = 4 * GDN_W + 2 * GDN_HEADS + 3 * FOX_W
            w_small = jnp.zeros((d, LANE), w.dtype)
            w_small = w_small.at[:, SM_F:SM_F + 4].set(w[:, o_fd:o_fd + 4])
            w_small = w_small.at[:, SM_B:SM_B + 4].set(w[:, 4 * GDN_W:4 * GDN_W + 4])
            w_small = w_small.at[:, SM_A:SM_A + 4].set(w[:, 4 * GDN_W + 4:4 * GDN_W + 8])
            w_re = jnp.concatenate([w[:, :4 * GDN_W], w[:, 4 * GDN_W + 8:o_fd], w_small], axis=1).astype(BF16)
            gdn_in, small, fox = _odd_inproj(xf, nw, w_re, seq)
            cw = jnp.zeros((8, 3 * GDN_W), F32).at[:GDN_CONV].set(gdn_conv_w[i].astype(F32))
            ya, c_col, c_row = _gdn(gdn_in, small, cw, _lane_row(gdn_a_log[i], SM_A), _lane_row(gdn_dt_bias[i], SM_A),
                                    _lane_row(fox_forget_bias[i], SM_F), row(gdn_norm_w[i]), batch, seq)
            yb = _fox_attention(fox, c_col, c_row, batch, seq)
        wo = w_out[l].astype(BF16)
        half = wo.shape[0] // 2
        xf = _outproj_ffn(xf, ya, yb, wo[:half], wo[half:], row(ffn_norm_w[l]), w_gate[l].astype(BF16),
                          w_up[l].astype(BF16), w_down[l].astype(BF16), row(final_norm_w), l == DEPTH - 1)
    return xf.reshape(batch, seq, d).astype(x.dtype)
```

```python
import functools
import math

import jax
import jax.numpy as jnp
from jax import lax
from jax.experimental import pallas as pl
from jax.experimental.pallas import tpu as pltpu

F32 = jnp.float32
BF16 = jnp.bfloat16

D_MODEL = 1024
DEPTH = 4
CONV_CH = 512
CONV_WIDTH = 31
DIFF_HEADS = 4
DIFF_HEAD_DIM = 64
ROPE_THETA = 500000.0
ROT_DIM = 16
GDN_HEADS = 4
GDN_HEAD_DIM = 128
GDN_W = 512
GDN_CONV = 4
FOX_HEADS = 4
FOX_W = 512
D_FF = 2816
LANE = 128
SUBLANE = 8
GDN_CHUNK = 128
GDN_CHUNKS_PER_STEP = 2
NEG_BIG = -1e30

TM_PROJ = 512
TM_FFN = 512
T_CONV = 512
CONV_HALO = 32
CONV_ROWS = 32
TQ_DIFF = 512
TQ_FOX = 512
TK_ATTN = 512
LOG2E = math.log2(math.e)
VMEM_LIMIT = 56 * 1024 * 1024


def _cparams(sem):
    return pltpu.CompilerParams(dimension_semantics=sem, vmem_limit_bytes=VMEM_LIMIT)


def _rms(x, w, eps=1e-6):
    return x * lax.rsqrt(jnp.mean(x * x, axis=-1, keepdims=True) + eps) * w


def _sigmoid(x):
    return 1.0 / (1.0 + jnp.exp(-x))


def _silu(x):
    return x * _sigmoid(x)


def _softplus(x):
    return jnp.maximum(x, 0.0) + jnp.log(1.0 + jnp.exp(-jnp.abs(x)))


def _dot(a, b):
    return jnp.dot(a, b, preferred_element_type=F32)


def _dot_nt(a, b):
    return lax.dot_general(a, b, (((1,), (1,)), ((), ())), preferred_element_type=F32)


def _split2(x):
    hi = x.astype(BF16)
    return hi, (x - hi.astype(F32)).astype(BF16)


def _dot3(a, b):
    a_hi, a_lo = a
    b_hi, b_lo = b
    return (_dot(jnp.concatenate([a_hi, a_lo], axis=1), jnp.concatenate([b_hi, b_hi], axis=0))
            + _dot(a_hi, b_lo))


def _cumsum_rows(ltri_b, x):
    hi = x.astype(BF16)
    r1 = x - hi.astype(F32)
    mid = r1.astype(BF16)
    lo = (r1 - mid.astype(F32)).astype(BF16)
    return _dot(jnp.concatenate([ltri_b, ltri_b, ltri_b], axis=1), jnp.concatenate([hi, mid, lo], axis=0))


def _even_inproj_kernel(x_ref, nw_ref, w_ref, c_ref, sa_ref, sb_ref, conv_ref, qkv_ref):
    hb = _rms(x_ref[...], nw_ref[...]).astype(BF16)
    conv_ref[...] = _dot(hb, w_ref[:, :2 * CONV_CH])
    qk = _dot(hb, w_ref[:, 2 * CONV_CH:2 * CONV_CH + 1024])
    c = c_ref[...]
    sa = sa_ref[...]
    sb = sb_ref[...]
    for j in range(8):
        blk = qk[:, LANE * j:LANE * (j + 1)]
        rot = blk * c + pltpu.roll(blk, LANE - ROT_DIM // 2, 1) * sa + pltpu.roll(blk, ROT_DIM // 2, 1) * sb
        if j < 4:
            rot = rot * (DIFF_HEAD_DIM ** -0.5 * LOG2E)
        qkv_ref[:, LANE * j:LANE * (j + 1)] = rot.astype(BF16)
    qkv_ref[:, 1024:] = _dot(hb, w_ref[:, 2 * CONV_CH + 1024:]).astype(BF16)


def _even_inproj(x, nw, w, rot_c, rot_sa, rot_sb, seq):
    m = x.shape[0]
    tm = min(TM_PROJ, seq)
    nrot = seq // tm
    n_in = w.shape[1]
    return pl.pallas_call(
        _even_inproj_kernel,
        grid=(m // tm,),
        in_specs=[
            pl.BlockSpec((tm, D_MODEL), lambda i: (i, 0)),
            pl.BlockSpec((1, D_MODEL), lambda i: (0, 0)),
            pl.BlockSpec((D_MODEL, n_in), lambda i: (0, 0), pipeline_mode=pl.Buffered(1)),
            pl.BlockSpec((tm, LANE), lambda i: (i % nrot, 0)),
            pl.BlockSpec((tm, LANE), lambda i: (i % nrot, 0)),
            pl.BlockSpec((tm, LANE), lambda i: (i % nrot, 0)),
        ],
        out_specs=[
            pl.BlockSpec((tm, 2 * CONV_CH), lambda i: (i, 0)),
            pl.BlockSpec((tm, 1536), lambda i: (i, 0)),
        ],
        out_shape=[
            jax.ShapeDtypeStruct((m, 2 * CONV_CH), F32),
            jax.ShapeDtypeStruct((m, 1536), BF16),
        ],
        compiler_params=_cparams(("parallel",)),
        name="even_inproj",
    )(x, nw, w, rot_c, rot_sa, rot_sb)


def _odd_inproj_kernel(x_ref, nw_ref, w_ref, gdn_ref, small_ref, fox_ref):
    hb = _rms(x_ref[...], nw_ref[...]).astype(BF16)
    gdn_ref[...] = _dot(hb, w_ref[:, :2048])
    y = _dot(hb, w_ref[:, 2048:3584])
    fox_ref[:, :FOX_W] = (y[:, :FOX_W] * (GDN_HEAD_DIM ** -0.5 * LOG2E)).astype(BF16)
    fox_ref[:, FOX_W:] = y[:, FOX_W:].astype(BF16)
    small_ref[...] = _dot(hb, w_ref[:, 3584:])


def _odd_inproj(x, nw, w, seq):
    m = x.shape[0]
    tm = min(TM_PROJ, seq)
    n_in = w.shape[1]
    return pl.pallas_call(
        _odd_inproj_kernel,
        grid=(m // tm,),
        in_specs=[
            pl.BlockSpec((tm, D_MODEL), lambda i: (i, 0)),
            pl.BlockSpec((1, D_MODEL), lambda i: (0, 0)),
            pl.BlockSpec((D_MODEL, n_in), lambda i: (0, 0), pipeline_mode=pl.Buffered(1)),
        ],
        out_specs=[
            pl.BlockSpec((tm, 2048), lambda i: (i, 0)),
            pl.BlockSpec((tm, LANE), lambda i: (i, 0)),
            pl.BlockSpec((tm, 1536), lambda i: (i, 0)),
        ],
        out_shape=[
            jax.ShapeDtypeStruct((m, 2048), F32),
            jax.ShapeDtypeStruct((m, LANE), F32),
            jax.ShapeDtypeStruct((m, 1536), BF16),
        ],
        compiler_params=_cparams(("parallel",)),
        name="odd_inproj",
    )(x, nw, w)


def _conv_kernel(val_ref, gate_ref, pval_ref, pgate_ref, cw_ref, cb_ref, lw_ref, lb_ref, o_ref, buf_ref, *, t):
    i = pl.program_id(1)
    pu = pval_ref[...] * _sigmoid(pgate_ref[...])
    buf_ref[0, 0:CONV_HALO, :] = jnp.where(i > 0, pu, 0.0)
    buf_ref[0, CONV_HALO:CONV_HALO + t, :] = val_ref[...] * _sigmoid(gate_ref[...])
    span = t + CONV_HALO - SUBLANE
    for s in range(1, SUBLANE):
        buf_ref[s, 0:span, :] = buf_ref[0, s:s + span, :]
    off = CONV_HALO - (CONV_WIDTH - 1)
    cb = cb_ref[...]
    lw = lw_ref[...]
    lb = lb_ref[...]
    for r in range(t // CONV_ROWS):
        r0 = r * CONV_ROWS
        acc = jnp.broadcast_to(cb, (CONV_ROWS, CONV_CH))
        for j in range(CONV_WIDTH):
            a, s = divmod(off + j, SUBLANE)
            acc = acc + cw_ref[j:j + 1, :] * buf_ref[s, r0 + a * SUBLANE:r0 + a * SUBLANE + CONV_ROWS, :]
        mu = jnp.mean(acc, axis=-1, keepdims=True)
        xc = acc - mu
        var = jnp.mean(xc * xc, axis=-1, keepdims=True)
        y = xc * lax.rsqrt(var + 1e-5) * lw + lb
        o_ref[r0:r0 + CONV_ROWS, :] = _silu(y).astype(o_ref.dtype)


def _conv_module(conv_in, cw, cb, lw, lb, batch, seq):
    m = conv_in.shape[0]
    t = min(T_CONV, seq)
    nt = seq // t
    hb = t // CONV_HALO

    def prev_map(col):
        return lambda b, i: (jnp.maximum((b * nt + i) * hb - 1, 0), col)

    return pl.pallas_call(
        functools.partial(_conv_kernel, t=t),
        grid=(batch, nt),
        in_specs=[
            pl.BlockSpec((t, CONV_CH), lambda b, i: (b * nt + i, 0)),
            pl.BlockSpec((t, CONV_CH), lambda b, i: (b * nt + i, 1)),
            pl.BlockSpec((CONV_HALO, CONV_CH), prev_map(0)),
            pl.BlockSpec((CONV_HALO, CONV_CH), prev_map(1)),
            pl.BlockSpec((CONV_HALO, CONV_CH), lambda b, i: (0, 0)),
            pl.BlockSpec((1, CONV_CH), lambda b, i: (0, 0)),
            pl.BlockSpec((1, CONV_CH), lambda b, i: (0, 0)),
            pl.BlockSpec((1, CONV_CH), lambda b, i: (0, 0)),
        ],
        out_specs=pl.BlockSpec((t, CONV_CH), lambda b, i: (b * nt + i, 0)),
        out_shape=jax.ShapeDtypeStruct((m, CONV_CH), BF16),
        scratch_shapes=[pltpu.VMEM((SUBLANE, CONV_HALO + t, CONV_CH), F32)],
        compiler_params=_cparams(("parallel", "parallel")),
        name="conv_module",
    )(conv_in, conv_in, conv_in, conv_in, cw, cb, lw, lb)


def _flash_scratch(rows, tk):
    return [pltpu.VMEM((rows, tk), F32), pltpu.VMEM((rows, tk), F32),
            pltpu.VMEM((rows, tk), BF16), pltpu.VMEM((rows, tk), BF16),
            pltpu.VMEM((rows, LANE), F32), pltpu.VMEM((rows, 2 * LANE), F32)]


def _flash(q_ref, k_ref, v_ref, scratch, qi, tq, tk, row_bias_ref, col_bias_fn):
    s_a, s_b, p_a, p_b, m_ref, acc_ref = scratch
    rows = q_ref.shape[0]
    nblk = tk // LANE
    n_full = (qi * tq) // tk
    ones = jnp.ones((tk, LANE), BF16)

    def blk(c):
        return slice(c * LANE, (c + 1) * LANE)

    def put_scores(s_ref, j):
        start = pl.multiple_of(j * tk, tk)
        s = _dot_nt(q_ref[...], k_ref[pl.ds(start, tk), :])
        if row_bias_ref is None:
            s_ref[...] = s
        else:
            cb = col_bias_fn(start)
            for c in range(nblk):
                s_ref[:, blk(c)] = s[:, blk(c)] + row_bias_ref[...] + cb[:, blk(c)]

    def pv(p_ref, j):
        v = v_ref[pl.ds(pl.multiple_of(j * tk, tk), tk), :]
        return _dot(p_ref[...], jnp.concatenate([v, ones], axis=1))

    def stage(s_cur, s_nxt, p_cur, p_prv, j, masked):
        if s_nxt is not None:
            put_scores(s_nxt, j + 1)
        pv_prev = pv(p_prv, jnp.maximum(j - 1, 0))

        def sblk(c):
            s = s_cur[:, blk(c)]
            if masked:
                rpos = lax.broadcasted_iota(jnp.int32, (rows, LANE), 0)
                if rows > tq:
                    rpos = jnp.where(rpos >= tq, rpos - tq, rpos)
                cpos = lax.broadcasted_iota(jnp.int32, (rows, LANE), 1) + (c * LANE + j * tk - qi * tq)
                s = jnp.where(cpos <= rpos, s, NEG_BIG)
            return s

        mx = sblk(0)
        for c in range(1, nblk):
            mx = jnp.maximum(mx, sblk(c))
        m_old = m_ref[...]
        m_new = jnp.maximum(m_old, jnp.broadcast_to(jnp.max(mx, axis=-1, keepdims=True), (rows, LANE)))
        alpha = jnp.exp2(m_old - m_new)
        m_ref[...] = m_new
        for c in range(nblk):
            p_cur[:, blk(c)] = jnp.exp2(sblk(c) - m_new).astype(BF16)
        for c in range(2):
            acc_ref[:, blk(c)] = (acc_ref[:, blk(c)] + pv_prev[:, blk(c)]) * alpha

    m_ref[...] = jnp.full(m_ref.shape, NEG_BIG, F32)
    acc_ref[...] = jnp.zeros(acc_ref.shape, F32)
    p_b[...] = jnp.zeros(p_b.shape, BF16)
    put_scores(s_a, 0)

    def quad(t, carry):
        for u in range(0, 4, 2):
            stage(s_a, s_b, p_a, p_b, 4 * t + u, False)
            stage(s_b, s_a, p_b, p_a, 4 * t + u + 1, False)
        return carry

    lax.fori_loop(0, n_full // 4, quad, 0)
    base = (n_full // 4) * 4

    def pair(t, carry):
        stage(s_a, s_b, p_a, p_b, base + 2 * t, False)
        stage(s_b, s_a, p_b, p_a, base + 2 * t + 1, False)
        return carry

    lax.fori_loop(0, (n_full - base) // 2, pair, 0)

    @pl.when(n_full % 2 == 0)
    def _():
        stage(s_a, None, p_a, p_b, n_full, True)
        acc_ref[...] += pv(p_a, n_full)

    @pl.when(n_full % 2 == 1)
    def _():
        stage(s_a, s_b, p_a, p_b, n_full - 1, False)
        stage(s_b, None, p_b, p_a, n_full, True)
        acc_ref[...] += pv(p_b, n_full)


def _diff_attn_kernel(q_ref, k_ref, v_ref, lq1_ref, lk1_ref, lq2_ref, lk2_ref, sw_ref, o_ref, qs_ref, *scratch,
                      tq, tk, lambda_init):
    qi = pl.program_id(2)
    q = q_ref[...]
    lane = lax.broadcasted_iota(jnp.int32, q.shape, 1)
    zero = jnp.zeros_like(q)
    qs_ref[0:tq, :] = jnp.where(lane < DIFF_HEAD_DIM, q, zero)
    qs_ref[tq:2 * tq, :] = jnp.where(lane >= DIFF_HEAD_DIM, q, zero)
    _flash(qs_ref, k_ref, v_ref, scratch, qi, tq, tk, None, None)
    acc_ref = scratch[-1]
    lam = (jnp.exp(jnp.sum(lq1_ref[...] * lk1_ref[...], axis=-1, keepdims=True))
           - jnp.exp(jnp.sum(lq2_ref[...] * lk2_ref[...], axis=-1, keepdims=True)) + lambda_init)
    o = (acc_ref[0:tq, 0:LANE] / acc_ref[0:tq, LANE:2 * LANE]
         - lam * (acc_ref[tq:2 * tq, 0:LANE] / acc_ref[tq:2 * tq, LANE:2 * LANE]))
    o_ref[...] = (_rms(o, sw_ref[...]) * (1.0 - lambda_init)).astype(o_ref.dtype)


def _diff_attention(qkv, lq1, lk1, lq2, lk2, sw, lambda_init, batch, seq):
    m = qkv.shape[0]
    tq = min(TQ_DIFF, seq)
    tk = min(TK_ATTN, seq)
    assert tk % tq == 0
    nq = seq // tq
    vec = pl.BlockSpec((1, DIFF_HEAD_DIM), lambda b, h, i: (0, 0))
    return pl.pallas_call(
        functools.partial(_diff_attn_kernel, tq=tq, tk=tk, lambda_init=lambda_init),
        grid=(batch, DIFF_HEADS, nq),
        in_specs=[
            pl.BlockSpec((tq, LANE), lambda b, h, i: (b * nq + i, h)),
            pl.BlockSpec((seq, LANE), lambda b, h, i: (b, 4 + h)),
            pl.BlockSpec((seq, LANE), lambda b, h, i: (b, 8 + h)),
            vec, vec, vec, vec,
            pl.BlockSpec((1, LANE), lambda b, h, i: (0, 0)),
        ],
        out_specs=pl.BlockSpec((tq, LANE), lambda b, h, i: (b * nq + i, h)),
        out_shape=jax.ShapeDtypeStruct((m, 4 * LANE), BF16),
        scratch_shapes=[pltpu.VMEM((2 * tq, LANE), BF16)] + _flash_scratch(2 * tq, tk),
        compiler_params=_cparams(("parallel", "parallel", "parallel")),
        name="diff_attention",
    )(qkv, qkv, qkv, lq1, lk1, lq2, lk2, sw)


def _fox_attn_kernel(q_ref, k_ref, v_ref, cq_ref, ck_ref, o_ref, kbias_ref, qbias_ref, *scratch, tq, tk):
    h = pl.program_id(1)
    qi = pl.program_id(2)
    cq_blk = cq_ref[...]
    lane = lax.broadcasted_iota(jnp.int32, cq_blk.shape, 1)
    cq = jnp.sum(jnp.where(lane == h, cq_blk, 0.0), axis=-1, keepdims=True)
    ck_blk = ck_ref[0]
    sub = lax.broadcasted_iota(jnp.int32, ck_blk.shape, 0)
    ck = jnp.sum(jnp.where(sub == h, ck_blk, 0.0), axis=0, keepdims=True)
    c_base = cq[0:1, :]
    kbias_ref[...] = (c_base - ck) * LOG2E

    def col_bias(start):
        return kbias_ref[:, pl.ds(start, tk)]

    qbias_ref[...] = jnp.broadcast_to((cq - c_base) * LOG2E, (tq, LANE))
    _flash(q_ref, k_ref, v_ref, scratch, qi, tq, tk, qbias_ref, col_bias)
    acc_ref = scratch[-1]
    o_ref[...] = (acc_ref[:, 0:LANE] / acc_ref[:, LANE:2 * LANE]).astype(o_ref.dtype)


def _fox_attention(fox, c_col, c_row, batch, seq):
    m = fox.shape[0]
    tq = min(TQ_FOX, seq)
    tk = min(TK_ATTN, seq)
    assert tk % tq == 0
    nq = seq // tq
    return pl.pallas_call(
        functools.partial(_fox_attn_kernel, tq=tq, tk=tk),
        grid=(batch, FOX_HEADS, nq),
        in_specs=[
            pl.BlockSpec((tq, LANE), lambda b, h, i: (b * nq + i, h)),
            pl.BlockSpec((seq, LANE), lambda b, h, i: (b, 4 + h)),
            pl.BlockSpec((seq, LANE), lambda b, h, i: (b, 8 + h)),
            pl.BlockSpec((tq, LANE), lambda b, h, i: (b * nq + i, 0)),
            pl.BlockSpec((1, 8, seq), lambda b, h, i: (b, 0, 0)),
        ],
        out_specs=pl.BlockSpec((tq, LANE), lambda b, h, i: (b * nq + i, h)),
        out_shape=jax.ShapeDtypeStruct((m, FOX_W), BF16),
        scratch_shapes=[pltpu.VMEM((1, seq), F32), pltpu.VMEM((tq, LANE), F32)] + _flash_scratch(tq, tk),
        compiler_params=_cparams(("parallel", "parallel", "parallel")),
        name="fox_attention",
    )(fox, fox, fox, c_col, c_row)


SM_F, SM_B, SM_A = 0, 4, 8


def _unit_lower_inverses(mats, row, col):
    size = mats[0].shape[0]
    eye = (row == col).astype(F32)
    shift = 4
    same = jnp.right_shift(row, shift) == jnp.right_shift(col, shift)
    n1 = [jnp.where(same, a, 0.0) for a in mats]
    xs = [eye - n for n in n1]
    ps = [_split2(n) for n in n1]
    for _ in range(shift - 1):
        ps = [_split2(_dot3(p, p)) for p in ps]
        xs = [x + _dot3(_split2(x), p) for x, p in zip(xs, ps)]
    while (1 << shift) < size:
        shift += 1
        wider = jnp.right_shift(row, shift) == jnp.right_shift(col, shift)
        rs = [_split2(jnp.where(wider, jnp.where(same, 0.0, a), 0.0)) for a in mats]
        xss = [_split2(x) for x in xs]
        ys = [_split2(_dot3(x2, r)) for x2, r in zip(xss, rs)]
        xs = [x - _dot3(y, x2) for x, y, x2 in zip(xs, ys, xss)]
        same = wider
    return xs


def _gdn_kernel(qkv_ref, prev_ref, z_ref, small_ref, cw_ref, alog_ref, dt_ref, fb_ref, gw_ref,
                o_ref, ccol_ref, crow_ref, state_ref, carry_ref, buf_ref, *, nch):
    i = pl.program_id(1)
    c = GDN_CHUNK
    t = nch * c

    @pl.when(i == 0)
    def _():
        state_ref[...] = jnp.zeros_like(state_ref)
        carry_ref[...] = jnp.zeros_like(carry_ref)

    buf_ref[0:8, :] = jnp.where(i > 0, prev_ref[...], 0.0)
    buf_ref[8:8 + t, :] = qkv_ref[...]
    off = 8 - (GDN_CONV - 1)
    y = cw_ref[0:1, :] * buf_ref[off:off + t, :]
    for j in range(1, GDN_CONV):
        y = y + cw_ref[j:j + 1, :] * buf_ref[off + j:off + j + t, :]
    y = _silu(y)

    row = lax.broadcasted_iota(jnp.int32, (c, c), 0)
    col = lax.broadcasted_iota(jnp.int32, (c, c), 1)
    causal = row >= col
    strict = row > col
    ltri_b = jnp.where(causal, 1.0, 0.0).astype(BF16)

    sm = small_ref[...]
    xf = sm + fb_ref[...]
    logf = jnp.minimum(xf, 0.0) - jnp.log(1.0 + jnp.exp(-jnp.abs(xf)))
    g = -jnp.exp(alog_ref[...]) * _softplus(sm + dt_ref[...])
    is_f = lax.broadcasted_iota(jnp.int32, (1, LANE), 1) < SM_B
    summand = jnp.where(is_f, logf, g)
    beta_all = _sigmoid(sm)
    gw = gw_ref[...]
    carry = carry_ref[...]
    gcs = []
    for ch in range(nch):
        gc = _cumsum_rows(ltri_b, summand[ch * c:(ch + 1) * c, :])
        cl = gc + carry
        carry = jnp.where(is_f, cl[c - 1:c, :], 0.0)
        ccol_ref[ch * c:(ch + 1) * c, :] = cl
        crow_ref[0, :, ch * c:(ch + 1) * c] = cl.T[0:8, :]
        gcs.append((gc, gc.T))
    carry_ref[...] = carry

    pairs = [(ch, h) for ch in range(nch) for h in range(GDN_HEADS)]
    pre = []
    for ch, h in pairs:
        rows = slice(ch * c, (ch + 1) * c)
        gc, gct = gcs[ch]
        qh = y[rows, h * LANE:(h + 1) * LANE]
        kh = y[rows, GDN_W + h * LANE:GDN_W + (h + 1) * LANE]
        vh = y[rows, 2 * GDN_W + h * LANE:2 * GDN_W + (h + 1) * LANE]
        qn = qh * (lax.rsqrt(jnp.sum(qh * qh, axis=-1, keepdims=True) + 1e-6) * (GDN_HEAD_DIM ** -0.5))
        kn = kh * lax.rsqrt(jnp.sum(kh * kh, axis=-1, keepdims=True) + 1e-6)
        gcol = gc[:, SM_A + h:SM_A + h + 1]
        grow = gct[SM_A + h:SM_A + h + 1, :]
        beta = beta_all[rows, SM_B + h:SM_B + h + 1]
        decay = jnp.where(causal, jnp.exp(jnp.where(causal, gcol - grow, 0.0)), 0.0)
        kb = kn * beta
        knb = kn.astype(BF16)
        a_mat = jnp.where(strict, _dot_nt(kb.astype(BF16), knb) * decay, 0.0)
        eg = jnp.exp(gcol)
        g_last = gcol[c - 1:c, :]
        pre.append(dict(
            a=a_mat,
            rhs=jnp.concatenate([vh * beta, kb * eg], axis=-1).astype(BF16),
            qk=jnp.where(causal, _dot_nt(qn.astype(BF16), knb) * decay, 0.0).astype(BF16),
            qeg=(qn * eg).astype(BF16),
            kdt=(kn * jnp.exp(g_last - gcol)).T.astype(BF16),
            sdec=jnp.exp(g_last)))
    t_invs = _unit_lower_inverses([p["a"] for p in pre], row, col)
    uws = [_dot(ti.astype(BF16), p["rhs"]) for ti, p in zip(t_invs, pre)]

    states = [state_ref[h] for h in range(GDN_HEADS)]
    for ch in range(nch):
        rows = slice(ch * c, (ch + 1) * c)
        idx = [ch * GDN_HEADS + h for h in range(GDN_HEADS)]
        sbs = [s.astype(BF16) for s in states]
        vnbs = [(uws[k][:, :LANE] - _dot(uws[k][:, LANE:].astype(BF16), sb)).astype(BF16)
                for k, sb in zip(idx, sbs)]
        outs = [_dot(pre[k]["qeg"], sb) + _dot(pre[k]["qk"], vnb) for k, sb, vnb in zip(idx, sbs, vnbs)]
        states = [s * pre[k]["sdec"] + _dot(pre[k]["kdt"], vnb) for k, s, vnb in zip(idx, states, vnbs)]
        for h, o in enumerate(outs):
            zh = z_ref[rows, h * LANE:(h + 1) * LANE]
            o_ref[rows, h * LANE:(h + 1) * LANE] = (_rms(o, gw) * _silu(zh)).astype(o_ref.dtype)
    for h in range(GDN_HEADS):
        state_ref[h] = states[h]


def _gdn(gdn_in, small, cw, alog, dt, fb, gw, batch, seq):
    m = gdn_in.shape[0]
    nch = min(GDN_CHUNKS_PER_STEP, seq // GDN_CHUNK)
    t = nch * GDN_CHUNK
    nt = seq // t
    vec = pl.BlockSpec((1, LANE), lambda b, i: (0, 0))
    return pl.pallas_call(
        functools.partial(_gdn_kernel, nch=nch),
        grid=(batch, nt),
        in_specs=[
            pl.BlockSpec((t, 3 * GDN_W), lambda b, i: (b * nt + i, 0)),
            pl.BlockSpec((8, 3 * GDN_W), lambda b, i: (jnp.maximum((b * nt + i) * (t // 8) - 1, 0), 0)),
            pl.BlockSpec((t, GDN_W), lambda b, i: (b * nt + i, 3)),
            pl.BlockSpec((t, LANE), lambda b, i: (b * nt + i, 0)),
            pl.BlockSpec((8, 3 * GDN_W), lambda b, i: (0, 0)),
            vec, vec, vec, vec,
        ],
        out_specs=[
            pl.BlockSpec((t, GDN_W), lambda b, i: (b * nt + i, 0)),
            pl.BlockSpec((t, LANE), lambda b, i: (b * nt + i, 0)),
            pl.BlockSpec((1, 8, t), lambda b, i: (b, 0, i)),
        ],
        out_shape=[
            jax.ShapeDtypeStruct((m, GDN_W), BF16),
            jax.ShapeDtypeStruct((m, LANE), F32),
            jax.ShapeDtypeStruct((batch, 8, seq), F32),
        ],
        scratch_shapes=[
            pltpu.VMEM((GDN_HEADS, GDN_HEAD_DIM, GDN_HEAD_DIM), F32),
            pltpu.VMEM((1, LANE), F32),
            pltpu.VMEM((8 + t, 3 * GDN_W), F32),
        ],
        compiler_params=_cparams(("parallel", "arbitrary")),
        name="gated_deltanet",
    )(gdn_in, gdn_in, gdn_in, small, cw, alog, dt, fb, gw)


def _outproj_ffn_kernel(x_ref, ya_ref, yb_ref, woa_ref, wob_ref, nw_ref, wg_ref, wu_ref, wd_ref, fw_ref, o_ref,
                        *, final_norm):
    x1 = x_ref[...] + _dot(ya_ref[...], woa_ref[...]) + _dot(yb_ref[...], wob_ref[...])
    hb = _rms(x1, nw_ref[...]).astype(BF16)
    g = _dot(hb, wg_ref[...])
    u = _dot(hb, wu_ref[...])
    a = (_silu(g) * u).astype(BF16)
    out = x1 + _dot(a, wd_ref[...])
    if final_norm:
        out = _rms(out, fw_ref[...])
    o_ref[...] = out


def _outproj_ffn(x, ya, yb, woa, wob, nw, wg, wu, wd, fw, final_norm):
    m = x.shape[0]
    tm = min(TM_FFN, m)
    half = ya.shape[1]

    def resident(shape):
        return pl.BlockSpec(shape, lambda i: (0, 0), pipeline_mode=pl.Buffered(1))

    return pl.pallas_call(
        functools.partial(_outproj_ffn_kernel, final_norm=final_norm),
        grid=(m // tm,),
        in_specs=[
            pl.BlockSpec((tm, D_MODEL), lambda i: (i, 0)),
            pl.BlockSpec((tm, half), lambda i: (i, 0)),
            pl.BlockSpec((tm, half), lambda i: (i, 0)),
            resident((half, D_MODEL)),
            resident((half, D_MODEL)),
            pl.BlockSpec((1, D_MODEL), lambda i: (0, 0)),
            resident((D_MODEL, D_FF)),
            resident((D_MODEL, D_FF)),
            resident((D_FF, D_MODEL)),
            pl.BlockSpec((1, D_MODEL), lambda i: (0, 0)),
        ],
        out_specs=pl.BlockSpec((tm, D_MODEL), lambda i: (i, 0)),
        out_shape=jax.ShapeDtypeStruct((m, D_MODEL), F32),
        compiler_params=_cparams(("parallel",)),
        name="outproj_ffn",
    )(x, ya, yb, woa, wob, nw, wg, wu, wd, fw)


def _rotary_tables(seq):
    half = ROT_DIM // 2
    inv_freq = 1.0 / (ROPE_THETA ** (jnp.arange(half, dtype=F32) / half))
    ang = jnp.arange(seq, dtype=F32)[:, None] * inv_freq[None, :]
    lane = jnp.arange(LANE)
    within = lane % DIFF_HEAD_DIM
    idx = within % half
    cos_l = jnp.cos(ang)[:, idx]
    sin_l = jnp.sin(ang)[:, idx]
    rot_c = jnp.where(within < ROT_DIM, cos_l, 1.0)
    rot_sa = jnp.where(within < half, -sin_l, 0.0)
    rot_sb = jnp.where((within >= half) & (within < ROT_DIM), sin_l, 0.0)
    return rot_c, rot_sa, rot_sb


def _lane_row(vals, offset):
    return jnp.zeros((1, LANE), F32).at[0, offset:offset + vals.shape[0]].set(vals.astype(F32))


def kernel(x, mix_norm_w, ffn_norm_w, final_norm_w, w_in_even, conv_w, conv_b, conv_ln_w, conv_ln_b, lambda_q1, lambda_k1, lambda_q2, lambda_k2, diff_subln_w, gdn_conv_w, w_in_odd, gdn_a_log, gdn_dt_bias, gdn_norm_w, fox_forget_bias, w_out, w_gate, w_up, w_down):
    batch, seq, d = x.shape
    m = batch * seq
    xf = x.reshape(m, d).astype(F32)
    rot_c, rot_sa, rot_sb = _rotary_tables(seq)
    row = lambda v: v.reshape(1, -1).astype(F32)

    for l in range(DEPTH):
        i = l // 2
        nw = row(mix_norm_w[l])
        if l % 2 == 0:
            lambda_init = 0.8 - 0.6 * math.exp(-0.3 * l)
            conv_in, qkv = _even_inproj(xf, nw, w_in_even[i].astype(BF16), rot_c, rot_sa, rot_sb, seq)
            cw = jnp.zeros((CONV_HALO, CONV_CH), F32).at[:CONV_WIDTH].set(conv_w[i].astype(F32))
            ya = _conv_module(conv_in, cw, row(conv_b[i]), row(conv_ln_w[i]), row(conv_ln_b[i]), batch, seq)
            yb = _diff_attention(qkv, row(lambda_q1[i]), row(lambda_k1[i]), row(lambda_q2[i]), row(lambda_k2[i]),
                                 row(diff_subln_w[i]), lambda_init, batch, seq)
        else:
            w = w_in_odd[i]
            o_fd = 4 * GDN_W + 2 * GDN_HEADS + 3 * FOX_W
            w_small = jnp.zeros((d, LANE), w.dtype)
            w_small = w_small.at[:, SM_F:SM_F + 4].set(w[:, o_fd:o_fd + 4])
            w_small = w_small.at[:, SM_B:SM_B + 4].set(w[:, 4 * GDN_W:4 * GDN_W + 4])
            w_small = w_small.at[:, SM_A:SM_A + 4].set(w[:, 4 * GDN_W + 4:4 * GDN_W + 8])
            w_re = jnp.concatenate([w[:, :4 * GDN_W], w[:, 4 * GDN_W + 8:o_fd], w_small], axis=1).astype(BF16)
            gdn_in, small, fox = _odd_inproj(xf, nw, w_re, seq)
            cw = jnp.zeros((8, 3 * GDN_W), F32).at[:GDN_CONV].set(gdn_conv_w[i].astype(F32))
            ya, c_col, c_row = _gdn(gdn_in, small, cw, _lane_row(gdn_a_log[i], SM_A), _lane_row(gdn_dt_bias[i], SM_A),
                                    _lane_row(fox_forget_bias[i], SM_F), row(gdn_norm_w[i]), batch, seq)
            yb = _fox_attention(fox, c_col, c_row, batch, seq)
        wo = w_out[l].astype(BF16)
        half = wo.shape[0] // 2
        xf = _outproj_ffn(xf, ya, yb, wo[:half], wo[half:], row(ffn_norm_w[l]), w_gate[l].astype(BF16),
                          w_up[l].astype(BF16), w_down[l].astype(BF16), row(final_norm_w), l == DEPTH - 1)
    return xf.reshape(batch, seq, d).astype(x.dtype)
```

```python
import functools
import math
from typing import NamedTuple

import jax
import jax.numpy as jnp
from jax import lax
from jax.experimental import pallas as pl
from jax.experimental.pallas import tpu as pltpu

F32 = jnp.float32
BF16 = jnp.bfloat16

D_MODEL = 1024
DEPTH = 4
CONV_CH = 512
CONV_WIDTH = 31
DIFF_HEADS = 4
DIFF_HEAD_DIM = 64
ROPE_THETA = 500000.0
ROT_DIM = 16
GDN_HEADS = 4
GDN_HEAD_DIM = 128
GDN_W = 512
GDN_CONV = 4
FOX_HEADS = 4
FOX_W = 512
D_FF = 2816
LANE = 128
SUBLANE = 8
GDN_CHUNK = 128
GDN_CHUNKS_PER_STEP = 2
NEG_BIG = -1e30

TM_PROJ = 512
TM_FFN = 512
T_CONV = 512
CONV_HALO = 32
CONV_ROWS = 32
TQ_DIFF = 256
ATTN_HEADS_PER_STEP = 2
TQ_FOX = 512
TK_ATTN = 512
LOG2E = math.log2(math.e)
VMEM_LIMIT = 56 * 1024 * 1024


def _cparams(sem):
    return pltpu.CompilerParams(dimension_semantics=sem, vmem_limit_bytes=VMEM_LIMIT)


def _rms(x, w, eps=1e-6):
    return x * lax.rsqrt(jnp.mean(x * x, axis=-1, keepdims=True) + eps) * w


def _sigmoid(x):
    return 1.0 / (1.0 + jnp.exp(-x))


def _silu(x):
    return x * _sigmoid(x)


def _softplus(x):
    return jnp.maximum(x, 0.0) + jnp.log(1.0 + jnp.exp(-jnp.abs(x)))


def _dot(a, b):
    return jnp.dot(a, b, preferred_element_type=F32)


def _dot_nt(a, b):
    return lax.dot_general(a, b, (((1,), (1,)), ((), ())), preferred_element_type=F32)


def _split2(x):
    hi = x.astype(BF16)
    return hi, (x - hi.astype(F32)).astype(BF16)


def _dot3(a, b):
    a_hi, a_lo = a
    b_hi, b_lo = b
    return (_dot(jnp.concatenate([a_hi, a_lo], axis=1), jnp.concatenate([b_hi, b_hi], axis=0))
            + _dot(a_hi, b_lo))


def _cumsum_rows(ltri_b, x):
    hi = x.astype(BF16)
    r1 = x - hi.astype(F32)
    mid = r1.astype(BF16)
    lo = (r1 - mid.astype(F32)).astype(BF16)
    return _dot(jnp.concatenate([ltri_b, ltri_b, ltri_b], axis=1), jnp.concatenate([hi, mid, lo], axis=0))


def _even_inproj_kernel(x_ref, nw_ref, w_ref, c_ref, sa_ref, sb_ref, conv_ref, qkv_ref):
    hb = _rms(x_ref[...], nw_ref[...]).astype(BF16)
    conv_ref[...] = _dot(hb, w_ref[:, :2 * CONV_CH])
    qk = _dot(hb, w_ref[:, 2 * CONV_CH:2 * CONV_CH + 1024])
    c = c_ref[...]
    sa = sa_ref[...]
    sb = sb_ref[...]
    for j in range(8):
        blk = qk[:, LANE * j:LANE * (j + 1)]
        rot = blk * c + pltpu.roll(blk, LANE - ROT_DIM // 2, 1) * sa + pltpu.roll(blk, ROT_DIM // 2, 1) * sb
        if j < 4:
            rot = rot * (DIFF_HEAD_DIM ** -0.5 * LOG2E)
        qkv_ref[:, LANE * j:LANE * (j + 1)] = rot.astype(BF16)
    qkv_ref[:, 1024:] = _dot(hb, w_ref[:, 2 * CONV_CH + 1024:]).astype(BF16)


def _even_inproj(x, nw, w, rot_c, rot_sa, rot_sb, seq):
    m = x.shape[0]
    tm = min(TM_PROJ, seq)
    nrot = seq // tm
    n_in = w.shape[1]
    return pl.pallas_call(
        _even_inproj_kernel,
        grid=(m // tm,),
        in_specs=[
            pl.BlockSpec((tm, D_MODEL), lambda i: (i, 0)),
            pl.BlockSpec((1, D_MODEL), lambda i: (0, 0)),
            pl.BlockSpec((D_MODEL, n_in), lambda i: (0, 0), pipeline_mode=pl.Buffered(1)),
            pl.BlockSpec((tm, LANE), lambda i: (i % nrot, 0)),
            pl.BlockSpec((tm, LANE), lambda i: (i % nrot, 0)),
            pl.BlockSpec((tm, LANE), lambda i: (i % nrot, 0)),
        ],
        out_specs=[
            pl.BlockSpec((tm, 2 * CONV_CH), lambda i: (i, 0)),
            pl.BlockSpec((tm, 1536), lambda i: (i, 0)),
        ],
        out_shape=[
            jax.ShapeDtypeStruct((m, 2 * CONV_CH), F32),
            jax.ShapeDtypeStruct((m, 1536), BF16),
        ],
        compiler_params=_cparams(("parallel",)),
        name="even_inproj",
    )(x, nw, w, rot_c, rot_sa, rot_sb)


def _odd_inproj_kernel(x_ref, nw_ref, w_ref, cw_ref, gdn_ref, small_ref, fox_ref, *bufs, tm, tiles_per_seq):
    i = pl.program_id(0)
    hb = _rms(x_ref[...], nw_ref[...]).astype(BF16)
    nqkv = 3 * GDN_W

    @pl.when(i % tiles_per_seq == 0)
    def _():
        for buf_ref in bufs:
            buf_ref[0:SUBLANE, :] = jnp.zeros((SUBLANE, GDN_W), F32)

    off = SUBLANE - (GDN_CONV - 1)

    def conv_silu(g):
        cols = slice(g * GDN_W, (g + 1) * GDN_W)
        proj = _dot(hb, w_ref[:, cols])
        ext = jnp.concatenate([bufs[g][...], proj], axis=0)
        conv = cw_ref[GDN_CONV - 1:GDN_CONV, cols] * proj
        for j in range(GDN_CONV - 1):
            conv = conv + cw_ref[j:j + 1, cols] * ext[off + j:off + j + tm, :]
        gdn_ref[:, cols] = _silu(conv)
        bufs[g][...] = proj[tm - SUBLANE:tm, :]

    for g in range(3):
        conv_silu(g)
    gdn_ref[:, nqkv:] = _dot(hb, w_ref[:, nqkv:2048])
    y = _dot(hb, w_ref[:, 2048:3584])
    fox_ref[:, :FOX_W] = (y[:, :FOX_W] * (GDN_HEAD_DIM ** -0.5 * LOG2E)).astype(BF16)
    fox_ref[:, FOX_W:] = y[:, FOX_W:].astype(BF16)
    small_ref[...] = _dot(hb, w_ref[:, 3584:])


def _odd_inproj(x, nw, w, cw, seq):
    m = x.shape[0]
    tm = min(TM_PROJ, seq)
    n_in = w.shape[1]
    return pl.pallas_call(
        functools.partial(_odd_inproj_kernel, tm=tm, tiles_per_seq=seq // tm),
        grid=(m // tm,),
        in_specs=[
            pl.BlockSpec((tm, D_MODEL), lambda i: (i, 0)),
            pl.BlockSpec((1, D_MODEL), lambda i: (0, 0)),
            pl.BlockSpec((D_MODEL, n_in), lambda i: (0, 0), pipeline_mode=pl.Buffered(1)),
            pl.BlockSpec((SUBLANE, 3 * GDN_W), lambda i: (0, 0)),
        ],
        out_specs=[
            pl.BlockSpec((tm, 2048), lambda i: (i, 0)),
            pl.BlockSpec((tm, LANE), lambda i: (i, 0)),
            pl.BlockSpec((tm, 1536), lambda i: (i, 0)),
        ],
        out_shape=[
            jax.ShapeDtypeStruct((m, 2048), F32),
            jax.ShapeDtypeStruct((m, LANE), F32),
            jax.ShapeDtypeStruct((m, 1536), BF16),
        ],
        scratch_shapes=[pltpu.VMEM((SUBLANE, GDN_W), F32)] * 3,
        compiler_params=_cparams(("arbitrary",)),
        name="odd_inproj",
    )(x, nw, w, cw)


def _conv_kernel(val_ref, gate_ref, pval_ref, pgate_ref, cw_ref, cb_ref, lw_ref, lb_ref, o_ref, buf_ref, *, t):
    i = pl.program_id(1)
    pu = pval_ref[...] * _sigmoid(pgate_ref[...])
    buf_ref[0, 0:CONV_HALO, :] = jnp.where(i > 0, pu, 0.0)
    buf_ref[0, CONV_HALO:CONV_HALO + t, :] = val_ref[...] * _sigmoid(gate_ref[...])
    span = t + CONV_HALO - SUBLANE
    for s in range(1, SUBLANE):
        buf_ref[s, 0:span, :] = buf_ref[0, s:s + span, :]
    off = CONV_HALO - (CONV_WIDTH - 1)
    cb = cb_ref[...]
    lw = lw_ref[...]
    lb = lb_ref[...]
    for r in range(t // CONV_ROWS):
        r0 = r * CONV_ROWS
        acc = jnp.broadcast_to(cb, (CONV_ROWS, CONV_CH))
        for j in range(CONV_WIDTH):
            a, s = divmod(off + j, SUBLANE)
            acc = acc + cw_ref[j:j + 1, :] * buf_ref[s, r0 + a * SUBLANE:r0 + a * SUBLANE + CONV_ROWS, :]
        mu = jnp.mean(acc, axis=-1, keepdims=True)
        xc = acc - mu
        var = jnp.mean(xc * xc, axis=-1, keepdims=True)
        y = xc * lax.rsqrt(var + 1e-5) * lw + lb
        o_ref[r0:r0 + CONV_ROWS, :] = _silu(y).astype(o_ref.dtype)


def _conv_module(conv_in, cw, cb, lw, lb, batch, seq):
    m = conv_in.shape[0]
    t = min(T_CONV, seq)
    nt = seq // t
    hb = t // CONV_HALO

    def prev_map(col):
        return lambda b, i: (jnp.maximum((b * nt + i) * hb - 1, 0), col)

    return pl.pallas_call(
        functools.partial(_conv_kernel, t=t),
        grid=(batch, nt),
        in_specs=[
            pl.BlockSpec((t, CONV_CH), lambda b, i: (b * nt + i, 0)),
            pl.BlockSpec((t, CONV_CH), lambda b, i: (b * nt + i, 1)),
            pl.BlockSpec((CONV_HALO, CONV_CH), prev_map(0)),
            pl.BlockSpec((CONV_HALO, CONV_CH), prev_map(1)),
            pl.BlockSpec((CONV_HALO, CONV_CH), lambda b, i: (0, 0)),
            pl.BlockSpec((1, CONV_CH), lambda b, i: (0, 0)),
            pl.BlockSpec((1, CONV_CH), lambda b, i: (0, 0)),
            pl.BlockSpec((1, CONV_CH), lambda b, i: (0, 0)),
        ],
        out_specs=pl.BlockSpec((t, CONV_CH), lambda b, i: (b * nt + i, 0)),
        out_shape=jax.ShapeDtypeStruct((m, CONV_CH), BF16),
        scratch_shapes=[pltpu.VMEM((SUBLANE, CONV_HALO + t, CONV_CH), F32)],
        compiler_params=_cparams(("parallel", "parallel")),
        name="conv_module",
    )(conv_in, conv_in, conv_in, conv_in, cw, cb, lw, lb)


def _flash_scratch(rows, tk):
    return [pltpu.VMEM((rows, tk), F32), pltpu.VMEM((rows, tk), F32),
            pltpu.VMEM((rows, tk), BF16), pltpu.VMEM((rows, tk), BF16),
            pltpu.VMEM((rows, LANE), F32), pltpu.VMEM((rows, 2 * LANE), F32)]


class _Stream(NamedTuple):
    q_ref: object
    k_ref: object
    v_ref: object
    lanes: slice
    scratch: tuple
    row_bias_ref: object = None
    col_bias_fn: object = None


def _flash(streams, qi, tq, tk):
    rows = streams[0].q_ref.shape[0]
    nblk = tk // LANE
    n_full = (qi * tq) // tk
    ones = jnp.ones((tk, LANE), BF16)

    def blk(c):
        return slice(c * LANE, (c + 1) * LANE)

    def put_scores(st, s_ref, j):
        start = pl.multiple_of(j * tk, tk)
        s = _dot_nt(st.q_ref[...], st.k_ref[pl.ds(start, tk), st.lanes])
        if st.row_bias_ref is None:
            s_ref[...] = s
        else:
            cb = st.col_bias_fn(start)
            for c in range(nblk):
                s_ref[:, blk(c)] = s[:, blk(c)] + st.row_bias_ref[...] + cb[:, blk(c)]

    def pv(st, p_ref, j):
        v = st.v_ref[pl.ds(pl.multiple_of(j * tk, tk), tk), st.lanes]
        return _dot(p_ref[...], jnp.concatenate([v, ones], axis=1))

    def softmax(st, s_cur, p_cur, pv_prev, j, masked):
        m_ref, acc_ref = st.scratch[4], st.scratch[5]

        def sblk(c):
            s = s_cur[:, blk(c)]
            if masked:
                rpos = lax.broadcasted_iota(jnp.int32, (rows, LANE), 0)
                if rows > tq:
                    rpos = jnp.where(rpos >= tq, rpos - tq, rpos)
                cpos = lax.broadcasted_iota(jnp.int32, (rows, LANE), 1) + (c * LANE + j * tk - qi * tq)
                s = jnp.where(cpos <= rpos, s, NEG_BIG)
            return s

        mx = sblk(0)
        for c in range(1, nblk):
            mx = jnp.maximum(mx, sblk(c))
        m_old = m_ref[...]
        m_new = jnp.maximum(m_old, jnp.broadcast_to(jnp.max(mx, axis=-1, keepdims=True), (rows, LANE)))
        alpha = jnp.exp2(m_old - m_new)
        m_ref[...] = m_new
        for c in range(nblk):
            p_cur[:, blk(c)] = jnp.exp2(sblk(c) - m_new).astype(BF16)
        for c in range(2):
            acc_ref[:, blk(c)] = (acc_ref[:, blk(c)] + pv_prev[:, blk(c)]) * alpha

    def stage(cur, j, masked, has_next=True):
        nxt = 1 - cur
        pvs = []
        for st in streams:
            if has_next:
                put_scores(st, st.scratch[nxt], j + 1)
            pvs.append(pv(st, st.scratch[2 + nxt], jnp.maximum(j - 1, 0)))
        for st, pv_prev in zip(streams, pvs):
            softmax(st, st.scratch[cur], st.scratch[2 + cur], pv_prev, j, masked)
            if not has_next:
                st.scratch[5][...] += pv(st, st.scratch[2 + cur], j)

    for st in streams:
        st.scratch[4][...] = jnp.full(st.scratch[4].shape, NEG_BIG, F32)
        st.scratch[5][...] = jnp.zeros(st.scratch[5].shape, F32)
        st.scratch[3][...] = jnp.zeros(st.scratch[3].shape, BF16)
        put_scores(st, st.scratch[0], 0)

    def quad(t, carry):
        for u in range(4):
            stage(u % 2, 4 * t + u, False)
        return carry

    lax.fori_loop(0, n_full // 4, quad, 0)
    base = (n_full // 4) * 4

    def pair(t, carry):
        stage(0, base + 2 * t, False)
        stage(1, base + 2 * t + 1, False)
        return carry

    lax.fori_loop(0, (n_full - base) // 2, pair, 0)

    @pl.when(n_full % 2 == 0)
    def _():
        stage(0, n_full, True, has_next=False)

    @pl.when(n_full % 2 == 1)
    def _():
        stage(0, n_full - 1, False)
        stage(1, n_full, True, has_next=False)


def _diff_attn_kernel(q_ref, k_ref, v_ref, lq1_ref, lk1_ref, lq2_ref, lk2_ref, sw_ref, o_ref, *scratch,
                      tq, tk, lambda_init):
    qi = pl.program_id(2)
    per = len(scratch) // ATTN_HEADS_PER_STEP
    streams = []
    for e in range(ATTN_HEADS_PER_STEP):
        lanes = slice(e * LANE, (e + 1) * LANE)
        qs_ref = scratch[e * per]
        q = q_ref[:, lanes]
        lane = lax.broadcasted_iota(jnp.int32, q.shape, 1)
        zero = jnp.zeros_like(q)
        qs_ref[0:tq, :] = jnp.where(lane < DIFF_HEAD_DIM, q, zero)
        qs_ref[tq:2 * tq, :] = jnp.where(lane >= DIFF_HEAD_DIM, q, zero)
        streams.append(_Stream(qs_ref, k_ref, v_ref, lanes, scratch[e * per + 1:(e + 1) * per]))
    _flash(streams, qi, tq, tk)
    lam = (jnp.exp(jnp.sum(lq1_ref[...] * lk1_ref[...], axis=-1, keepdims=True))
           - jnp.exp(jnp.sum(lq2_ref[...] * lk2_ref[...], axis=-1, keepdims=True)) + lambda_init)
    for st in streams:
        acc_ref = st.scratch[5]
        o = (acc_ref[0:tq, 0:LANE] / acc_ref[0:tq, LANE:2 * LANE]
             - lam * (acc_ref[tq:2 * tq, 0:LANE] / acc_ref[tq:2 * tq, LANE:2 * LANE]))
        o_ref[:, st.lanes] = (_rms(o, sw_ref[...]) * (1.0 - lambda_init)).astype(o_ref.dtype)


def _diff_attention(qkv, lq1, lk1, lq2, lk2, sw, lambda_init, batch, seq):
    m = qkv.shape[0]
    tq = min(TQ_DIFF, seq)
    tk = min(TK_ATTN, seq)
    assert tk % tq == 0
    nq = seq // tq
    hps = ATTN_HEADS_PER_STEP
    wide = hps * LANE
    ng = DIFF_HEADS // hps
    vec = pl.BlockSpec((1, DIFF_HEAD_DIM), lambda b, g, i: (0, 0))
    return pl.pallas_call(
        functools.partial(_diff_attn_kernel, tq=tq, tk=tk, lambda_init=lambda_init),
        grid=(batch, ng, nq),
        in_specs=[
            pl.BlockSpec((tq, wide), lambda b, g, i: (b * nq + i, g)),
            pl.BlockSpec((seq, wide), lambda b, g, i: (b, ng + g)),
            pl.BlockSpec((seq, wide), lambda b, g, i: (b, 2 * ng + g)),
            vec, vec, vec, vec,
            pl.BlockSpec((1, LANE), lambda b, g, i: (0, 0)),
        ],
        out_specs=pl.BlockSpec((tq, wide), lambda b, g, i: (b * nq + i, g)),
        out_shape=jax.ShapeDtypeStruct((m, 4 * LANE), BF16),
        scratch_shapes=([pltpu.VMEM((2 * tq, LANE), BF16)] + _flash_scratch(2 * tq, tk)) * hps,
        compiler_params=_cparams(("parallel", "parallel", "parallel")),
        name="diff_attention",
    )(qkv, qkv, qkv, lq1, lk1, lq2, lk2, sw)


def _fox_attn_kernel(q_ref, k_ref, v_ref, cq_ref, ck_ref, o_ref, *scratch, tq, tk):
    g = pl.program_id(1)
    qi = pl.program_id(2)
    per = len(scratch) // ATTN_HEADS_PER_STEP
    cq_blk = cq_ref[...]
    lane = lax.broadcasted_iota(jnp.int32, cq_blk.shape, 1)
    ck_blk = ck_ref[0]
    sub = lax.broadcasted_iota(jnp.int32, ck_blk.shape, 0)
    streams = []
    for e in range(ATTN_HEADS_PER_STEP):
        h = g * ATTN_HEADS_PER_STEP + e
        kbias_ref, qbias_ref = scratch[e * per], scratch[e * per + 1]
        cq = jnp.sum(jnp.where(lane == h, cq_blk, 0.0), axis=-1, keepdims=True)
        ck = jnp.sum(jnp.where(sub == h, ck_blk, 0.0), axis=0, keepdims=True)
        c_base = cq[0:1, :]
        kbias_ref[...] = (c_base - ck) * LOG2E
        qbias_ref[...] = jnp.broadcast_to((cq - c_base) * LOG2E, (tq, LANE))
        streams.append(_Stream(q_ref.at[:, e * LANE:(e + 1) * LANE], k_ref, v_ref, slice(e * LANE, (e + 1) * LANE),
                               scratch[e * per + 2:(e + 1) * per], qbias_ref,
                               functools.partial(lambda ref, start: ref[:, pl.ds(start, tk)], kbias_ref)))
    _flash(streams, qi, tq, tk)
    for st in streams:
        acc_ref = st.scratch[5]
        o_ref[:, st.lanes] = (acc_ref[:, 0:LANE] / acc_ref[:, LANE:2 * LANE]).astype(o_ref.dtype)


def _fox_attention(fox, c_col, c_row, batch, seq):
    m = fox.shape[0]
    tq = min(TQ_FOX, seq)
    tk = min(TK_ATTN, seq)
    assert tk % tq == 0
    nq = seq // tq
    hps = ATTN_HEADS_PER_STEP
    wide = hps * LANE
    ng = FOX_HEADS // hps
    return pl.pallas_call(
        functools.partial(_fox_attn_kernel, tq=tq, tk=tk),
        grid=(batch, ng, nq),
        in_specs=[
            pl.BlockSpec((tq, wide), lambda b, g, i: (b * nq + i, g)),
            pl.BlockSpec((seq, wide), lambda b, g, i: (b, ng + g)),
            pl.BlockSpec((seq, wide), lambda b, g, i: (b, 2 * ng + g)),
            pl.BlockSpec((tq, LANE), lambda b, g, i: (b * nq + i, 0)),
            pl.BlockSpec((1, 8, seq), lambda b, g, i: (b, 0, 0)),
        ],
        out_specs=pl.BlockSpec((tq, wide), lambda b, g, i: (b * nq + i, g)),
        out_shape=jax.ShapeDtypeStruct((m, FOX_W), BF16),
        scratch_shapes=([pltpu.VMEM((1, seq), F32), pltpu.VMEM((tq, LANE), F32)] + _flash_scratch(tq, tk)) * hps,
        compiler_params=_cparams(("parallel", "parallel", "parallel")),
        name="fox_attention",
    )(fox, fox, fox, c_col, c_row)


SM_F, SM_B, SM_A = 0, 4, 8


def _unit_lower_inverses(mats, row, col):
    size = mats[0].shape[0]
    eye = (row == col).astype(F32)
    shift = 4
    same = jnp.right_shift(row, shift) == jnp.right_shift(col, shift)
    zero = jnp.zeros(mats[0].shape, BF16)
    splits = [_split2(a) for a in mats]
    xs = [eye - jnp.where(same, a, 0.0) for a in mats]
    ps = [(jnp.where(same, hi, zero), jnp.where(same, lo, zero)) for hi, lo in splits]
    for _ in range(shift - 1):
        ps = [_split2(_dot3(p, p)) for p in ps]
        xs = [x + _dot3(_split2(x), p) for x, p in zip(xs, ps)]
    while (1 << shift) < size:
        shift += 1
        wider = jnp.right_shift(row, shift) == jnp.right_shift(col, shift)
        ring = jnp.logical_and(wider, jnp.logical_not(same))
        rs = [(jnp.where(ring, hi, zero), jnp.where(ring, lo, zero)) for hi, lo in splits]
        xss = [_split2(x) for x in xs]
        ys = [_split2(_dot3(x2, r)) for x2, r in zip(xss, rs)]
        xs = [x - _dot3(y, x2) for x, y, x2 in zip(xs, ys, xss)]
        same = wider
    return xs


def _gdn_kernel(qkv_ref, z_ref, small_ref, alog_ref, dt_ref, fb_ref, gw_ref,
                o_ref, ccol_ref, crow_ref, state_ref, carry_ref, *, nch):
    i = pl.program_id(1)
    c = GDN_CHUNK

    @pl.when(i == 0)
    def _():
        state_ref[...] = jnp.zeros_like(state_ref)
        carry_ref[...] = jnp.zeros_like(carry_ref)

    y = qkv_ref

    row = lax.broadcasted_iota(jnp.int32, (c, c), 0)
    col = lax.broadcasted_iota(jnp.int32, (c, c), 1)
    causal = row >= col
    strict = row > col
    ltri_b = jnp.where(causal, 1.0, 0.0).astype(BF16)

    sm = small_ref[...]
    xf = sm + fb_ref[...]
    logf = jnp.minimum(xf, 0.0) - jnp.log(1.0 + jnp.exp(-jnp.abs(xf)))
    g = -jnp.exp(alog_ref[...]) * _softplus(sm + dt_ref[...])
    is_f = lax.broadcasted_iota(jnp.int32, (1, LANE), 1) < SM_B
    summand = jnp.where(is_f, logf, g)
    beta_all = _sigmoid(sm)
    gw = gw_ref[...]
    carry = carry_ref[...]
    gcs = []
    for ch in range(nch):
        gc = _cumsum_rows(ltri_b, summand[ch * c:(ch + 1) * c, :])
        cl = gc + carry
        carry = jnp.where(is_f, cl[c - 1:c, :], 0.0)
        ccol_ref[ch * c:(ch + 1) * c, :] = cl
        crow_ref[0, :, ch * c:(ch + 1) * c] = cl.T[0:8, :]
        gcs.append((gc, gc.T))
    carry_ref[...] = carry

    pairs = [(ch, h) for ch in range(nch) for h in range(GDN_HEADS)]
    pre = []
    for ch, h in pairs:
        rows = slice(ch * c, (ch + 1) * c)
        gc, gct = gcs[ch]
        qh = y[rows, h * LANE:(h + 1) * LANE]
        kh = y[rows, GDN_W + h * LANE:GDN_W + (h + 1) * LANE]
        vh = y[rows, 2 * GDN_W + h * LANE:2 * GDN_W + (h + 1) * LANE]
        qn = qh * (lax.rsqrt(jnp.sum(qh * qh, axis=-1, keepdims=True) + 1e-6) * (GDN_HEAD_DIM ** -0.5))
        kn = kh * lax.rsqrt(jnp.sum(kh * kh, axis=-1, keepdims=True) + 1e-6)
        gcol = gc[:, SM_A + h:SM_A + h + 1]
        grow = gct[SM_A + h:SM_A + h + 1, :]
        beta = beta_all[rows, SM_B + h:SM_B + h + 1]
        decay = jnp.where(causal, jnp.exp(jnp.where(causal, gcol - grow, 0.0)), 0.0)
        kb = kn * beta
        knb = kn.astype(BF16)
        a_mat = jnp.where(strict, _dot_nt(kb.astype(BF16), knb) * decay, 0.0)
        eg = jnp.exp(gcol)
        g_last = gcol[c - 1:c, :]
        pre.append(dict(
            a=a_mat,
            rhs=jnp.concatenate([vh * beta, kb * eg], axis=-1).astype(BF16),
            qk=jnp.where(causal, _dot_nt(qn.astype(BF16), knb) * decay, 0.0).astype(BF16),
            qeg=(qn * eg).astype(BF16),
            kdt=(kn * jnp.exp(g_last - gcol)).T.astype(BF16),
            sdec=jnp.exp(g_last)))
    t_invs = _unit_lower_inverses([p["a"] for p in pre], row, col)
    uws = [_dot(ti.astype(BF16), p["rhs"]) for ti, p in zip(t_invs, pre)]

    states = [state_ref[h] for h in range(GDN_HEADS)]
    for ch in range(nch):
        rows = slice(ch * c, (ch + 1) * c)
        idx = [ch * GDN_HEADS + h for h in range(GDN_HEADS)]
        sbs = [s.astype(BF16) for s in states]
        vnbs = [(uws[k][:, :LANE] - _dot(uws[k][:, LANE:].astype(BF16), sb)).astype(BF16)
                for k, sb in zip(idx, sbs)]
        outs = [_dot(pre[k]["qeg"], sb) + _dot(pre[k]["qk"], vnb) for k, sb, vnb in zip(idx, sbs, vnbs)]
        states = [s * pre[k]["sdec"] + _dot(pre[k]["kdt"], vnb) for k, s, vnb in zip(idx, states, vnbs)]
        for h, o in enumerate(outs):
            zh = z_ref[rows, h * LANE:(h + 1) * LANE]
            o_ref[rows, h * LANE:(h + 1) * LANE] = (_rms(o, gw) * _silu(zh)).astype(o_ref.dtype)
    for h in range(GDN_HEADS):
        state_ref[h] = states[h]


def _gdn(gdn_in, small, alog, dt, fb, gw, batch, seq):
    m = gdn_in.shape[0]
    nch = min(GDN_CHUNKS_PER_STEP, seq // GDN_CHUNK)
    t = nch * GDN_CHUNK
    nt = seq // t
    vec = pl.BlockSpec((1, LANE), lambda b, i: (0, 0))
    return pl.pallas_call(
        functools.partial(_gdn_kernel, nch=nch),
        grid=(batch, nt),
        in_specs=[
            pl.BlockSpec((t, 3 * GDN_W), lambda b, i: (b * nt + i, 0)),
            pl.BlockSpec((t, GDN_W), lambda b, i: (b * nt + i, 3)),
            pl.BlockSpec((t, LANE), lambda b, i: (b * nt + i, 0)),
            vec, vec, vec, vec,
        ],
        out_specs=[
            pl.BlockSpec((t, GDN_W), lambda b, i: (b * nt + i, 0)),
            pl.BlockSpec((t, LANE), lambda b, i: (b * nt + i, 0)),
            pl.BlockSpec((1, 8, t), lambda b, i: (b, 0, i)),
        ],
        out_shape=[
            jax.ShapeDtypeStruct((m, GDN_W), BF16),
            jax.ShapeDtypeStruct((m, LANE), F32),
            jax.ShapeDtypeStruct((batch, 8, seq), F32),
        ],
        scratch_shapes=[
            pltpu.VMEM((GDN_HEADS, GDN_HEAD_DIM, GDN_HEAD_DIM), F32),
            pltpu.VMEM((1, LANE), F32),
        ],
        compiler_params=_cparams(("parallel", "arbitrary")),
        name="gated_deltanet",
    )(gdn_in, gdn_in, small, alog, dt, fb, gw)


def _outproj_ffn_kernel(x_ref, ya_ref, yb_ref, woa_ref, wob_ref, nw_ref, wg_ref, wu_ref, wd_ref, fw_ref, o_ref,
                        *, final_norm):
    x1 = x_ref[...] + _dot(ya_ref[...], woa_ref[...]) + _dot(yb_ref[...], wob_ref[...])
    hb = _rms(x1, nw_ref[...]).astype(BF16)
    g = _dot(hb, wg_ref[...])
    u = _dot(hb, wu_ref[...])
    a = (_silu(g) * u).astype(BF16)
    out = x1 + _dot(a, wd_ref[...])
    if final_norm:
        out = _rms(out, fw_ref[...])
    o_ref[...] = out


def _outproj_ffn(x, ya, yb, woa, wob, nw, wg, wu, wd, fw, final_norm):
    m = x.shape[0]
    tm = min(TM_FFN, m)
    half = ya.shape[1]

    def resident(shape):
        return pl.BlockSpec(shape, lambda i: (0, 0), pipeline_mode=pl.Buffered(1))

    return pl.pallas_call(
        functools.partial(_outproj_ffn_kernel, final_norm=final_norm),
        grid=(m // tm,),
        in_specs=[
            pl.BlockSpec((tm, D_MODEL), lambda i: (i, 0)),
            pl.BlockSpec((tm, half), lambda i: (i, 0)),
            pl.BlockSpec((tm, half), lambda i: (i, 0)),
            resident((half, D_MODEL)),
            resident((half, D_MODEL)),
            pl.BlockSpec((1, D_MODEL), lambda i: (0, 0)),
            resident((D_MODEL, D_FF)),
            resident((D_MODEL, D_FF)),
            resident((D_FF, D_MODEL)),
            pl.BlockSpec((1, D_MODEL), lambda i: (0, 0)),
        ],
        out_specs=pl.BlockSpec((tm, D_MODEL), lambda i: (i, 0)),
        out_shape=jax.ShapeDtypeStruct((m, D_MODEL), F32),
        compiler_params=_cparams(("parallel",)),
        name="outproj_ffn",
    )(x, ya, yb, woa, wob, nw, wg, wu, wd, fw)


def _rotary_tables(seq):
    half = ROT_DIM // 2
    inv_freq = 1.0 / (ROPE_THETA ** (jnp.arange(half, dtype=F32) / half))
    ang = jnp.arange(seq, dtype=F32)[:, None] * inv_freq[None, :]
    lane = jnp.arange(LANE)
    within = lane % DIFF_HEAD_DIM
    idx = within % half
    cos_l = jnp.cos(ang)[:, idx]
    sin_l = jnp.sin(ang)[:, idx]
    rot_c = jnp.where(within < ROT_DIM, cos_l, 1.0)
    rot_sa = jnp.where(within < half, -sin_l, 0.0)
    rot_sb = jnp.where((within >= half) & (within < ROT_DIM), sin_l, 0.0)
    return rot_c, rot_sa, rot_sb


def _lane_row(vals, offset):
    return jnp.zeros((1, LANE), F32).at[0, offset:offset + vals.shape[0]].set(vals.astype(F32))


def kernel(x, mix_norm_w, ffn_norm_w, final_norm_w, w_in_even, conv_w, conv_b, conv_ln_w, conv_ln_b, lambda_q1, lambda_k1, lambda_q2, lambda_k2, diff_subln_w, gdn_conv_w, w_in_odd, gdn_a_log, gdn_dt_bias, gdn_norm_w, fox_forget_bias, w_out, w_gate, w_up, w_down):
    batch, seq, d = x.shape
    m = batch * seq
    xf = x.reshape(m, d).astype(F32)
    rot_c, rot_sa, rot_sb = _rotary_tables(seq)
    row = lambda v: v.reshape(1, -1).astype(F32)

    for l in range(DEPTH):
        i = l // 2
        nw = row(mix_norm_w[l])
        if l % 2 == 0:
            lambda_init = 0.8 - 0.6 * math.exp(-0.3 * l)
            conv_in, qkv = _even_inproj(xf, nw, w_in_even[i].astype(BF16), rot_c, rot_sa, rot_sb, seq)
            cw = jnp.zeros((CONV_HALO, CONV_CH), F32).at[:CONV_WIDTH].set(conv_w[i].astype(F32))
            ya = _conv_module(conv_in, cw, row(conv_b[i]), row(conv_ln_w[i]), row(conv_ln_b[i]), batch, seq)
            yb = _diff_attention(qkv, row(lambda_q1[i]), row(lambda_k1[i]), row(lambda_q2[i]), row(lambda_k2[i]),
                                 row(diff_subln_w[i]), lambda_init, batch, seq)
        else:
            w = w_in_odd[i]
            o_fd = 4 * GDN_W + 2 * GDN_HEADS + 3 * FOX_W
            w_small = jnp.zeros((d, LANE), w.dtype)
            w_small = w_small.at[:, SM_F:SM_F + 4].set(w[:, o_fd:o_fd + 4])
            w_small = w_small.at[:, SM_B:SM_B + 4].set(w[:, 4 * GDN_W:4 * GDN_W + 4])
            w_small = w_small.at[:, SM_A:SM_A + 4].set(w[:, 4 * GDN_W + 4:4 * GDN_W + 8])
            w_re = jnp.concatenate([w[:, :4 * GDN_W], w[:, 4 * GDN_W + 8:o_fd], w_small], axis=1).astype(BF16)
            cw = jnp.zeros((SUBLANE, 3 * GDN_W), F32).at[:GDN_CONV].set(gdn_conv_w[i].astype(F32))
            gdn_in, small, fox = _odd_inproj(xf, nw, w_re, cw, seq)
            ya, c_col, c_row = _gdn(gdn_in, small, _lane_row(gdn_a_log[i], SM_A), _lane_row(gdn_dt_bias[i], SM_A),
                                    _lane_row(fox_forget_bias[i], SM_F), row(gdn_norm_w[i]), batch, seq)
            yb = _fox_attention(fox, c_col, c_row, batch, seq)
        wo = w_out[l].astype(BF16)
        half = wo.shape[0] // 2
        xf = _outproj_ffn(xf, ya, yb, wo[:half], wo[half:], row(ffn_norm_w[l]), w_gate[l].astype(BF16),
                          w_up[l].astype(BF16), w_down[l].astype(BF16), row(final_norm_w), l == DEPTH - 1)
    return xf.reshape(batch, seq, d).astype(x.dtype)
```

```python
import functools
import math
from typing import NamedTuple

import jax
import jax.numpy as jnp
from jax import lax
from jax.experimental import pallas as pl
from jax.experimental.pallas import tpu as pltpu

F32 = jnp.float32
BF16 = jnp.bfloat16

D_MODEL = 1024
DEPTH = 4
CONV_CH = 512
CONV_WIDTH = 31
DIFF_HEADS = 4
DIFF_HEAD_DIM = 64
ROPE_THETA = 500000.0
ROT_DIM = 16
GDN_HEADS = 4
GDN_HEAD_DIM = 128
GDN_W = 512
GDN_CONV = 4
FOX_HEADS = 4
FOX_W = 512
D_FF = 2816
LANE = 128
SUBLANE = 8
GDN_CHUNK = 128
GDN_CHUNKS_PER_STEP = 2
NEG_BIG = -1e30

TM_PROJ = 512
TM_FFN = 512
T_CONV = 512
CONV_HALO = 32
CONV_ROWS = 32
TQ_DIFF = 256
ATTN_HEADS_PER_STEP = 2
TQ_FOX = 512
TK_ATTN = 512
LOG2E = math.log2(math.e)
VMEM_LIMIT = 56 * 1024 * 1024


def _cparams(sem):
    return pltpu.CompilerParams(dimension_semantics=sem, vmem_limit_bytes=VMEM_LIMIT)


def _rms(x, w, eps=1e-6):
    return x * lax.rsqrt(jnp.mean(x * x, axis=-1, keepdims=True) + eps) * w


def _sigmoid(x):
    return 1.0 / (1.0 + jnp.exp(-x))


def _silu(x):
    return x * _sigmoid(x)


def _softplus(x):
    return jnp.maximum(x, 0.0) + jnp.log(1.0 + jnp.exp(-jnp.abs(x)))


def _dot(a, b):
    return jnp.dot(a, b, preferred_element_type=F32)


def _dot_nt(a, b):
    return lax.dot_general(a, b, (((1,), (1,)), ((), ())), preferred_element_type=F32)


def _split2(x):
    hi = x.astype(BF16)
    return hi, (x - hi.astype(F32)).astype(BF16)


def _dot3(a, b):
    a_hi, a_lo = a
    b_hi, b_lo = b
    return (_dot(jnp.concatenate([a_hi, a_lo], axis=1), jnp.concatenate([b_hi, b_hi], axis=0))
            + _dot(a_hi, b_lo))


def _cumsum_rows(ltri_b, x):
    hi = x.astype(BF16)
    r1 = x - hi.astype(F32)
    mid = r1.astype(BF16)
    lo = (r1 - mid.astype(F32)).astype(BF16)
    return _dot(jnp.concatenate([ltri_b, ltri_b, ltri_b], axis=1), jnp.concatenate([hi, mid, lo], axis=0))


def _even_inproj_kernel(x_ref, nw_ref, w_ref, c_ref, sa_ref, sb_ref, conv_ref, qkv_ref):
    hb = _rms(x_ref[...], nw_ref[...]).astype(BF16)
    conv_ref[...] = _dot(hb, w_ref[:, :2 * CONV_CH])
    qk = _dot(hb, w_ref[:, 2 * CONV_CH:2 * CONV_CH + 1024])
    c = c_ref[...]
    sa = sa_ref[...]
    sb = sb_ref[...]
    for j in range(8):
        blk = qk[:, LANE * j:LANE * (j + 1)]
        rot = blk * c + pltpu.roll(blk, LANE - ROT_DIM // 2, 1) * sa + pltpu.roll(blk, ROT_DIM // 2, 1) * sb
        if j < 4:
            rot = rot * (DIFF_HEAD_DIM ** -0.5 * LOG2E)
        qkv_ref[:, LANE * j:LANE * (j + 1)] = rot.astype(BF16)
    qkv_ref[:, 1024:] = _dot(hb, w_ref[:, 2 * CONV_CH + 1024:]).astype(BF16)


def _even_inproj(x, nw, w, rot_c, rot_sa, rot_sb, seq):
    m = x.shape[0]
    tm = min(TM_PROJ, seq)
    nrot = seq // tm
    n_in = w.shape[1]
    return pl.pallas_call(
        _even_inproj_kernel,
        grid=(m // tm,),
        in_specs=[
            pl.BlockSpec((tm, D_MODEL), lambda i: (i, 0)),
            pl.BlockSpec((1, D_MODEL), lambda i: (0, 0)),
            pl.BlockSpec((D_MODEL, n_in), lambda i: (0, 0), pipeline_mode=pl.Buffered(1)),
            pl.BlockSpec((tm, LANE), lambda i: (i % nrot, 0)),
            pl.BlockSpec((tm, LANE), lambda i: (i % nrot, 0)),
            pl.BlockSpec((tm, LANE), lambda i: (i % nrot, 0)),
        ],
        out_specs=[
            pl.BlockSpec((tm, 2 * CONV_CH), lambda i: (i, 0)),
            pl.BlockSpec((tm, 1536), lambda i: (i, 0)),
        ],
        out_shape=[
            jax.ShapeDtypeStruct((m, 2 * CONV_CH), F32),
            jax.ShapeDtypeStruct((m, 1536), BF16),
        ],
        compiler_params=_cparams(("parallel",)),
        name="even_inproj",
    )(x, nw, w, rot_c, rot_sa, rot_sb)


def _odd_inproj_kernel(x_ref, nw_ref, w_ref, cw_ref, gdn_ref, small_ref, fox_ref, *bufs, tm, tiles_per_seq):
    i = pl.program_id(0)
    hb = _rms(x_ref[...], nw_ref[...]).astype(BF16)
    nqkv = 3 * GDN_W

    @pl.when(i % tiles_per_seq == 0)
    def _():
        for buf_ref in bufs:
            buf_ref[0:SUBLANE, :] = jnp.zeros((SUBLANE, GDN_W), F32)

    off = SUBLANE - (GDN_CONV - 1)

    def conv_silu(g):
        cols = slice(g * GDN_W, (g + 1) * GDN_W)
        proj = _dot(hb, w_ref[:, cols])
        ext = jnp.concatenate([bufs[g][...], proj], axis=0)
        conv = cw_ref[GDN_CONV - 1:GDN_CONV, cols] * proj
        for j in range(GDN_CONV - 1):
            conv = conv + cw_ref[j:j + 1, cols] * ext[off + j:off + j + tm, :]
        gdn_ref[:, cols] = _silu(conv)
        bufs[g][...] = proj[tm - SUBLANE:tm, :]

    for g in range(3):
        conv_silu(g)
    gdn_ref[:, nqkv:] = _dot(hb, w_ref[:, nqkv:2048])
    y = _dot(hb, w_ref[:, 2048:3584])
    fox_ref[:, :FOX_W] = (y[:, :FOX_W] * (GDN_HEAD_DIM ** -0.5 * LOG2E)).astype(BF16)
    fox_ref[:, FOX_W:] = y[:, FOX_W:].astype(BF16)
    small_ref[...] = _dot(hb, w_ref[:, 3584:])


def _odd_inproj(x, nw, w, cw, seq):
    m = x.shape[0]
    tm = min(TM_PROJ, seq)
    n_in = w.shape[1]
    return pl.pallas_call(
        functools.partial(_odd_inproj_kernel, tm=tm, tiles_per_seq=seq // tm),
        grid=(m // tm,),
        in_specs=[
            pl.BlockSpec((tm, D_MODEL), lambda i: (i, 0)),
            pl.BlockSpec((1, D_MODEL), lambda i: (0, 0)),
            pl.BlockSpec((D_MODEL, n_in), lambda i: (0, 0), pipeline_mode=pl.Buffered(1)),
            pl.BlockSpec((SUBLANE, 3 * GDN_W), lambda i: (0, 0)),
        ],
        out_specs=[
            pl.BlockSpec((tm, 2048), lambda i: (i, 0)),
            pl.BlockSpec((tm, LANE), lambda i: (i, 0)),
            pl.BlockSpec((tm, 1536), lambda i: (i, 0)),
        ],
        out_shape=[
            jax.ShapeDtypeStruct((m, 2048), F32),
            jax.ShapeDtypeStruct((m, LANE), F32),
            jax.ShapeDtypeStruct((m, 1536), BF16),
        ],
        scratch_shapes=[pltpu.VMEM((SUBLANE, GDN_W), F32)] * 3,
        compiler_params=_cparams(("arbitrary",)),
        name="odd_inproj",
    )(x, nw, w, cw)


def _conv_kernel(val_ref, gate_ref, pval_ref, pgate_ref, cw_ref, cb_ref, lw_ref, lb_ref, o_ref, buf_ref, *, t):
    i = pl.program_id(1)
    pu = pval_ref[...] * _sigmoid(pgate_ref[...])
    buf_ref[0, 0:CONV_HALO, :] = jnp.where(i > 0, pu, 0.0)
    buf_ref[0, CONV_HALO:CONV_HALO + t, :] = val_ref[...] * _sigmoid(gate_ref[...])
    span = t + CONV_HALO - SUBLANE
    for s in range(1, SUBLANE):
        buf_ref[s, 0:span, :] = buf_ref[0, s:s + span, :]
    off = CONV_HALO - (CONV_WIDTH - 1)
    cb = cb_ref[...]
    lw = lw_ref[...]
    lb = lb_ref[...]
    for r in range(t // CONV_ROWS):
        r0 = r * CONV_ROWS
        acc = jnp.broadcast_to(cb, (CONV_ROWS, CONV_CH))
        for j in range(CONV_WIDTH):
            a, s = divmod(off + j, SUBLANE)
            acc = acc + cw_ref[j:j + 1, :] * buf_ref[s, r0 + a * SUBLANE:r0 + a * SUBLANE + CONV_ROWS, :]
        mu = jnp.mean(acc, axis=-1, keepdims=True)
        xc = acc - mu
        var = jnp.mean(xc * xc, axis=-1, keepdims=True)
        y = xc * lax.rsqrt(var + 1e-5) * lw + lb
        o_ref[r0:r0 + CONV_ROWS, :] = _silu(y).astype(o_ref.dtype)


def _conv_module(conv_in, cw, cb, lw, lb, batch, seq):
    m = conv_in.shape[0]
    t = min(T_CONV, seq)
    nt = seq // t
    hb = t // CONV_HALO

    def prev_map(col):
        return lambda b, i: (jnp.maximum((b * nt + i) * hb - 1, 0), col)

    return pl.pallas_call(
        functools.partial(_conv_kernel, t=t),
        grid=(batch, nt),
        in_specs=[
            pl.BlockSpec((t, CONV_CH), lambda b, i: (b * nt + i, 0)),
            pl.BlockSpec((t, CONV_CH), lambda b, i: (b * nt + i, 1)),
            pl.BlockSpec((CONV_HALO, CONV_CH), prev_map(0)),
            pl.BlockSpec((CONV_HALO, CONV_CH), prev_map(1)),
            pl.BlockSpec((CONV_HALO, CONV_CH), lambda b, i: (0, 0)),
            pl.BlockSpec((1, CONV_CH), lambda b, i: (0, 0)),
            pl.BlockSpec((1, CONV_CH), lambda b, i: (0, 0)),
            pl.BlockSpec((1, CONV_CH), lambda b, i: (0, 0)),
        ],
        out_specs=pl.BlockSpec((t, CONV_CH), lambda b, i: (b * nt + i, 0)),
        out_shape=jax.ShapeDtypeStruct((m, CONV_CH), BF16),
        scratch_shapes=[pltpu.VMEM((SUBLANE, CONV_HALO + t, CONV_CH), F32)],
        compiler_params=_cparams(("parallel", "parallel")),
        name="conv_module",
    )(conv_in, conv_in, conv_in, conv_in, cw, cb, lw, lb)


def _flash_scratch(rows, tk):
    return [pltpu.VMEM((rows, tk), F32), pltpu.VMEM((rows, tk), F32),
            pltpu.VMEM((rows, tk), BF16), pltpu.VMEM((rows, tk), BF16),
            pltpu.VMEM((rows, LANE), F32), pltpu.VMEM((rows, 2 * LANE), F32)]


class _Stream(NamedTuple):
    q_ref: object
    k_ref: object
    v_ref: object
    lanes: slice
    scratch: tuple
    row_bias_ref: object = None
    col_bias_fn: object = None


def _flash(streams, qi, tq, tk):
    rows = streams[0].q_ref.shape[0]
    nblk = tk // LANE
    n_full = (qi * tq) // tk
    ones = jnp.ones((tk, LANE), BF16)

    def blk(c):
        return slice(c * LANE, (c + 1) * LANE)

    def put_scores(st, s_ref, j):
        start = pl.multiple_of(j * tk, tk)
        s = _dot_nt(st.q_ref[...], st.k_ref[pl.ds(start, tk), st.lanes])
        if st.row_bias_ref is None:
            s_ref[...] = s
        else:
            cb = st.col_bias_fn(start)
            for c in range(nblk):
                s_ref[:, blk(c)] = s[:, blk(c)] + st.row_bias_ref[...] + cb[:, blk(c)]

    def pv(st, p_ref, j):
        v = st.v_ref[pl.ds(pl.multiple_of(j * tk, tk), tk), st.lanes]
        return _dot(p_ref[...], jnp.concatenate([v, ones], axis=1))

    def softmax(st, s_cur, p_cur, pv_prev, j, masked):
        m_ref, acc_ref = st.scratch[4], st.scratch[5]

        def sblk(c):
            s = s_cur[:, blk(c)]
            if masked:
                rpos = lax.broadcasted_iota(jnp.int32, (rows, LANE), 0)
                if rows > tq:
                    rpos = jnp.where(rpos >= tq, rpos - tq, rpos)
                cpos = lax.broadcasted_iota(jnp.int32, (rows, LANE), 1) + (c * LANE + j * tk - qi * tq)
                s = jnp.where(cpos <= rpos, s, NEG_BIG)
            return s

        mx = sblk(0)
        for c in range(1, nblk):
            mx = jnp.maximum(mx, sblk(c))
        mx = jnp.broadcast_to(jnp.max(mx, axis=-1, keepdims=True), (rows, LANE))
        if pv_prev is None:
            m_new = mx
        else:
            m_old = m_ref[...]
            m_new = jnp.maximum(m_old, mx)
            alpha = jnp.exp2(m_old - m_new)
        m_ref[...] = m_new
        for c in range(nblk):
            p_cur[:, blk(c)] = jnp.exp2(sblk(c) - m_new).astype(BF16)
        if pv_prev is not None:
            for c in range(2):
                acc_ref[:, blk(c)] = (acc_ref[:, blk(c)] + pv_prev[:, blk(c)]) * alpha

    def stage(cur, j, masked, has_next=True, first=False):
        nxt = 1 - cur
        pvs = []
        for st in streams:
            if has_next:
                put_scores(st, st.scratch[nxt], j + 1)
            pvs.append(None if first else pv(st, st.scratch[2 + nxt], j - 1))
        for st, pv_prev in zip(streams, pvs):
            softmax(st, st.scratch[cur], st.scratch[2 + cur], pv_prev, j, masked)
            if not has_next:
                st.scratch[5][...] += pv(st, st.scratch[2 + cur], j)

    for st in streams:
        st.scratch[5][...] = jnp.zeros(st.scratch[5].shape, F32)
        put_scores(st, st.scratch[0], 0)

    @pl.when(n_full == 0)
    def _():
        stage(0, 0, True, has_next=False, first=True)

    @pl.when(n_full > 0)
    def _():
        stage(0, 0, False, first=True)

    rest = jnp.maximum(n_full - 1, 0)

    def quad(t, carry):
        for u in range(4):
            stage((1 + u) % 2, 1 + 4 * t + u, False)
        return carry

    lax.fori_loop(0, rest // 4, quad, 0)
    base = 1 + (rest // 4) * 4

    def pair(t, carry):
        stage(1, base + 2 * t, False)
        stage(0, base + 2 * t + 1, False)
        return carry

    lax.fori_loop(0, (rest % 4) // 2, pair, 0)

    @pl.when(jnp.logical_and(n_full > 0, rest % 2 == 0))
    def _():
        stage(1, n_full, True, has_next=False)

    @pl.when(rest % 2 == 1)
    def _():
        stage(1, n_full - 1, False)
        stage(0, n_full, True, has_next=False)


def _diff_attn_kernel(q_ref, k_ref, v_ref, lq1_ref, lk1_ref, lq2_ref, lk2_ref, sw_ref, o_ref, *scratch,
                      tq, tk, lambda_init):
    qi = pl.program_id(2)
    per = len(scratch) // ATTN_HEADS_PER_STEP
    streams = []
    for e in range(ATTN_HEADS_PER_STEP):
        lanes = slice(e * LANE, (e + 1) * LANE)
        qs_ref = scratch[e * per]
        q = q_ref[:, lanes]
        lane = lax.broadcasted_iota(jnp.int32, q.shape, 1)
        zero = jnp.zeros_like(q)
        qs_ref[0:tq, :] = jnp.where(lane < DIFF_HEAD_DIM, q, zero)
        qs_ref[tq:2 * tq, :] = jnp.where(lane >= DIFF_HEAD_DIM, q, zero)
        streams.append(_Stream(qs_ref, k_ref, v_ref, lanes, scratch[e * per + 1:(e + 1) * per]))
    _flash(streams, qi, tq, tk)
    lam = (jnp.exp(jnp.sum(lq1_ref[...] * lk1_ref[...], axis=-1, keepdims=True))
           - jnp.exp(jnp.sum(lq2_ref[...] * lk2_ref[...], axis=-1, keepdims=True)) + lambda_init)
    for st in streams:
        acc_ref = st.scratch[5]
        o = (acc_ref[0:tq, 0:LANE] / acc_ref[0:tq, LANE:2 * LANE]
             - lam * (acc_ref[tq:2 * tq, 0:LANE] / acc_ref[tq:2 * tq, LANE:2 * LANE]))
        o_ref[:, st.lanes] = (_rms(o, sw_ref[...]) * (1.0 - lambda_init)).astype(o_ref.dtype)


def _diff_attention(qkv, lq1, lk1, lq2, lk2, sw, lambda_init, batch, seq):
    m = qkv.shape[0]
    tq = min(TQ_DIFF, seq)
    tk = min(TK_ATTN, seq)
    assert tk % tq == 0
    nq = seq // tq
    hps = ATTN_HEADS_PER_STEP
    wide = hps * LANE
    ng = DIFF_HEADS // hps
    vec = pl.BlockSpec((1, DIFF_HEAD_DIM), lambda b, g, i: (0, 0))
    return pl.pallas_call(
        functools.partial(_diff_attn_kernel, tq=tq, tk=tk, lambda_init=lambda_init),
        grid=(batch, ng, nq),
        in_specs=[
            pl.BlockSpec((tq, wide), lambda b, g, i: (b * nq + i, g)),
            pl.BlockSpec((seq, wide), lambda b, g, i: (b, ng + g)),
            pl.BlockSpec((seq, wide), lambda b, g, i: (b, 2 * ng + g)),
            vec, vec, vec, vec,
            pl.BlockSpec((1, LANE), lambda b, g, i: (0, 0)),
        ],
        out_specs=pl.BlockSpec((tq, wide), lambda b, g, i: (b * nq + i, g)),
        out_shape=jax.ShapeDtypeStruct((m, 4 * LANE), BF16),
        scratch_shapes=([pltpu.VMEM((2 * tq, LANE), BF16)] + _flash_scratch(2 * tq, tk)) * hps,
        compiler_params=_cparams(("parallel", "parallel", "parallel")),
        name="diff_attention",
    )(qkv, qkv, qkv, lq1, lk1, lq2, lk2, sw)


def _fox_attn_kernel(q_ref, k_ref, v_ref, cq_ref, ck_ref, o_ref, *scratch, tq, tk):
    g = pl.program_id(1)
    qi = pl.program_id(2)
    per = len(scratch) // ATTN_HEADS_PER_STEP
    cq_blk = cq_ref[...]
    lane = lax.broadcasted_iota(jnp.int32, cq_blk.shape, 1)
    ck_blk = ck_ref[0]
    sub = lax.broadcasted_iota(jnp.int32, ck_blk.shape, 0)
    streams = []
    for e in range(ATTN_HEADS_PER_STEP):
        h = g * ATTN_HEADS_PER_STEP + e
        kbias_ref, qbias_ref = scratch[e * per], scratch[e * per + 1]
        cq = jnp.sum(jnp.where(lane == h, cq_blk, 0.0), axis=-1, keepdims=True)
        ck = jnp.sum(jnp.where(sub == h, ck_blk, 0.0), axis=0, keepdims=True)
        c_base = cq[0:1, :]
        kbias_ref[...] = (c_base - ck) * LOG2E
        qbias_ref[...] = jnp.broadcast_to((cq - c_base) * LOG2E, (tq, LANE))
        streams.append(_Stream(q_ref.at[:, e * LANE:(e + 1) * LANE], k_ref, v_ref, slice(e * LANE, (e + 1) * LANE),
                               scratch[e * per + 2:(e + 1) * per], qbias_ref,
                               functools.partial(lambda ref, start: ref[:, pl.ds(start, tk)], kbias_ref)))
    _flash(streams, qi, tq, tk)
    for st in streams:
        acc_ref = st.scratch[5]
        o_ref[:, st.lanes] = (acc_ref[:, 0:LANE] / acc_ref[:, LANE:2 * LANE]).astype(o_ref.dtype)


def _fox_attention(fox, c_col, c_row, batch, seq):
    m = fox.shape[0]
    tq = min(TQ_FOX, seq)
    tk = min(TK_ATTN, seq)
    assert tk % tq == 0
    nq = seq // tq
    hps = ATTN_HEADS_PER_STEP
    wide = hps * LANE
    ng = FOX_HEADS // hps
    return pl.pallas_call(
        functools.partial(_fox_attn_kernel, tq=tq, tk=tk),
        grid=(batch, ng, nq),
        in_specs=[
            pl.BlockSpec((tq, wide), lambda b, g, i: (b * nq + i, g)),
            pl.BlockSpec((seq, wide), lambda b, g, i: (b, ng + g)),
            pl.BlockSpec((seq, wide), lambda b, g, i: (b, 2 * ng + g)),
            pl.BlockSpec((tq, LANE), lambda b, g, i: (b * nq + i, 0)),
            pl.BlockSpec((1, 8, seq), lambda b, g, i: (b, 0, 0)),
        ],
        out_specs=pl.BlockSpec((tq, wide), lambda b, g, i: (b * nq + i, g)),
        out_shape=jax.ShapeDtypeStruct((m, FOX_W), BF16),
        scratch_shapes=([pltpu.VMEM((1, seq), F32), pltpu.VMEM((tq, LANE), F32)] + _flash_scratch(tq, tk)) * hps,
        compiler_params=_cparams(("parallel", "parallel", "parallel")),
        name="fox_attention",
    )(fox, fox, fox, c_col, c_row)


SM_F, SM_B, SM_A = 0, 4, 8


def _unit_lower_inverses(mats, row, col):
    size = mats[0].shape[0]
    eye = (row == col).astype(F32)
    shift = 4
    same = jnp.right_shift(row, shift) == jnp.right_shift(col, shift)
    zero = jnp.zeros(mats[0].shape, BF16)
    splits = [_split2(a) for a in mats]
    xs = [eye - jnp.where(same, a, 0.0) for a in mats]
    ps = [(jnp.where(same, hi, zero), jnp.where(same, lo, zero)) for hi, lo in splits]
    for _ in range(shift - 1):
        ps = [_split2(_dot3(p, p)) for p in ps]
        xs = [x + _dot3(_split2(x), p) for x, p in zip(xs, ps)]
    while (1 << shift) < size:
        shift += 1
        wider = jnp.right_shift(row, shift) == jnp.right_shift(col, shift)
        ring = jnp.logical_and(wider, jnp.logical_not(same))
        rs = [(jnp.where(ring, hi, zero), jnp.where(ring, lo, zero)) for hi, lo in splits]
        xss = [_split2(x) for x in xs]
        ys = [_split2(_dot3(x2, r)) for x2, r in zip(xss, rs)]
        xs = [x - _dot3(y, x2) for x, y, x2 in zip(xs, ys, xss)]
        same = wider
    return xs


def _gdn_kernel(qkv_ref, z_ref, small_ref, alog_ref, dt_ref, fb_ref, gw_ref,
                o_ref, ccol_ref, crow_ref, state_ref, carry_ref, *, nch):
    i = pl.program_id(1)
    c = GDN_CHUNK

    @pl.when(i == 0)
    def _():
        state_ref[...] = jnp.zeros_like(state_ref)
        carry_ref[...] = jnp.zeros_like(carry_ref)

    y = qkv_ref

    row = lax.broadcasted_iota(jnp.int32, (c, c), 0)
    col = lax.broadcasted_iota(jnp.int32, (c, c), 1)
    causal = row >= col
    strict = row > col
    ltri_b = jnp.where(causal, 1.0, 0.0).astype(BF16)

    sm = small_ref[...]
    xf = sm + fb_ref[...]
    logf = jnp.minimum(xf, 0.0) - jnp.log(1.0 + jnp.exp(-jnp.abs(xf)))
    g = -jnp.exp(alog_ref[...]) * _softplus(sm + dt_ref[...])
    is_f = lax.broadcasted_iota(jnp.int32, (1, LANE), 1) < SM_B
    summand = jnp.where(is_f, logf, g)
    beta_all = _sigmoid(sm)
    gw = gw_ref[...]
    carry = carry_ref[...]
    gcs = []
    for ch in range(nch):
        gc = _cumsum_rows(ltri_b, summand[ch * c:(ch + 1) * c, :])
        cl = gc + carry
        carry = jnp.where(is_f, cl[c - 1:c, :], 0.0)
        ccol_ref[ch * c:(ch + 1) * c, :] = cl
        crow_ref[0, :, ch * c:(ch + 1) * c] = cl.T[0:8, :]
        gcs.append((gc, gc.T))
    carry_ref[...] = carry

    pairs = [(ch, h) for ch in range(nch) for h in range(GDN_HEADS)]
    pre = []
    for ch, h in pairs:
        rows = slice(ch * c, (ch + 1) * c)
        gc, gct = gcs[ch]
        qh = y[rows, h * LANE:(h + 1) * LANE]
        kh = y[rows, GDN_W + h * LANE:GDN_W + (h + 1) * LANE]
        vh = y[rows, 2 * GDN_W + h * LANE:2 * GDN_W + (h + 1) * LANE]
        qn = qh * (lax.rsqrt(jnp.sum(qh * qh, axis=-1, keepdims=True) + 1e-6) * (GDN_HEAD_DIM ** -0.5))
        kn = kh * lax.rsqrt(jnp.sum(kh * kh, axis=-1, keepdims=True) + 1e-6)
        gcol = gc[:, SM_A + h:SM_A + h + 1]
        grow = gct[SM_A + h:SM_A + h + 1, :]
        beta = beta_all[rows, SM_B + h:SM_B + h + 1]
        decay = jnp.where(causal, jnp.exp(jnp.where(causal, gcol - grow, 0.0)), 0.0)
        kb = kn * beta
        knb = kn.astype(BF16)
        a_mat = jnp.where(strict, _dot_nt(kb.astype(BF16), knb) * decay, 0.0)
        eg = jnp.exp(gcol)
        g_last = gcol[c - 1:c, :]
        pre.append(dict(
            a=a_mat,
            rhs=jnp.concatenate([vh * beta, kb * eg], axis=-1).astype(BF16),
            qk=jnp.where(causal, _dot_nt(qn.astype(BF16), knb) * decay, 0.0).astype(BF16),
            qeg=(qn * eg).astype(BF16),
            kdt=(kn * jnp.exp(g_last - gcol)).T.astype(BF16),
            sdec=jnp.exp(g_last)))
    t_invs = _unit_lower_inverses([p["a"] for p in pre], row, col)
    uws = [_dot(ti.astype(BF16), p["rhs"]) for ti, p in zip(t_invs, pre)]

    states = [state_ref[h] for h in range(GDN_HEADS)]
    for ch in range(nch):
        rows = slice(ch * c, (ch + 1) * c)
        idx = [ch * GDN_HEADS + h for h in range(GDN_HEADS)]
        sbs = [s.astype(BF16) for s in states]
        vnbs = [(uws[k][:, :LANE] - _dot(uws[k][:, LANE:].astype(BF16), sb)).astype(BF16)
                for k, sb in zip(idx, sbs)]
        outs = [_dot(pre[k]["qeg"], sb) + _dot(pre[k]["qk"], vnb) for k, sb, vnb in zip(idx, sbs, vnbs)]
        states = [s * pre[k]["sdec"] + _dot(pre[k]["kdt"], vnb) for k, s, vnb in zip(idx, states, vnbs)]
        for h, o in enumerate(outs):
            zh = z_ref[rows, h * LANE:(h + 1) * LANE]
            o_ref[rows, h * LANE:(h + 1) * LANE] = (_rms(o, gw) * _silu(zh)).astype(o_ref.dtype)
    for h in range(GDN_HEADS):
        state_ref[h] = states[h]


def _gdn(gdn_in, small, alog, dt, fb, gw, batch, seq):
    m = gdn_in.shape[0]
    nch = min(GDN_CHUNKS_PER_STEP, seq // GDN_CHUNK)
    t = nch * GDN_CHUNK
    nt = seq // t
    vec = pl.BlockSpec((1, LANE), lambda b, i: (0, 0))
    return pl.pallas_call(
        functools.partial(_gdn_kernel, nch=nch),
        grid=(batch, nt),
        in_specs=[
            pl.BlockSpec((t, 3 * GDN_W), lambda b, i: (b * nt + i, 0)),
            pl.BlockSpec((t, GDN_W), lambda b, i: (b * nt + i, 3)),
            pl.BlockSpec((t, LANE), lambda b, i: (b * nt + i, 0)),
            vec, vec, vec, vec,
        ],
        out_specs=[
            pl.BlockSpec((t, GDN_W), lambda b, i: (b * nt + i, 0)),
            pl.BlockSpec((t, LANE), lambda b, i: (b * nt + i, 0)),
            pl.BlockSpec((1, 8, t), lambda b, i: (b, 0, i)),
        ],
        out_shape=[
            jax.ShapeDtypeStruct((m, GDN_W), BF16),
            jax.ShapeDtypeStruct((m, LANE), F32),
            jax.ShapeDtypeStruct((batch, 8, seq), F32),
        ],
        scratch_shapes=[
            pltpu.VMEM((GDN_HEADS, GDN_HEAD_DIM, GDN_HEAD_DIM), F32),
            pltpu.VMEM((1, LANE), F32),
        ],
        compiler_params=_cparams(("parallel", "arbitrary")),
        name="gated_deltanet",
    )(gdn_in, gdn_in, small, alog, dt, fb, gw)


def _outproj_ffn_kernel(x_ref, ya_ref, yb_ref, woa_ref, wob_ref, nw_ref, wg_ref, wu_ref, wd_ref, fw_ref, o_ref,
                        *, final_norm):
    x1 = x_ref[...] + _dot(ya_ref[...], woa_ref[...]) + _dot(yb_ref[...], wob_ref[...])
    hb = _rms(x1, nw_ref[...]).astype(BF16)
    g = _dot(hb, wg_ref[...])
    u = _dot(hb, wu_ref[...])
    a = (_silu(g) * u).astype(BF16)
    out = x1 + _dot(a, wd_ref[...])
    if final_norm:
        out = _rms(out, fw_ref[...])
    o_ref[...] = out


def _outproj_ffn(x, ya, yb, woa, wob, nw, wg, wu, wd, fw, final_norm):
    m = x.shape[0]
    tm = min(TM_FFN, m)
    half = ya.shape[1]

    def resident(shape):
        return pl.BlockSpec(shape, lambda i: (0, 0), pipeline_mode=pl.Buffered(1))

    return pl.pallas_call(
        functools.partial(_outproj_ffn_kernel, final_norm=final_norm),
        grid=(m // tm,),
        in_specs=[
            pl.BlockSpec((tm, D_MODEL), lambda i: (i, 0)),
            pl.BlockSpec((tm, half), lambda i: (i, 0)),
            pl.BlockSpec((tm, half), lambda i: (i, 0)),
            resident((half, D_MODEL)),
            resident((half, D_MODEL)),
            pl.BlockSpec((1, D_MODEL), lambda i: (0, 0)),
            resident((D_MODEL, D_FF)),
            resident((D_MODEL, D_FF)),
            resident((D_FF, D_MODEL)),
            pl.BlockSpec((1, D_MODEL), lambda i: (0, 0)),
        ],
        out_specs=pl.BlockSpec((tm, D_MODEL), lambda i: (i, 0)),
        out_shape=jax.ShapeDtypeStruct((m, D_MODEL), F32),
        compiler_params=_cparams(("parallel",)),
        name="outproj_ffn",
    )(x, ya, yb, woa, wob, nw, wg, wu, wd, fw)


def _rotary_tables(seq):
    half = ROT_DIM // 2
    inv_freq = 1.0 / (ROPE_THETA ** (jnp.arange(half, dtype=F32) / half))
    ang = jnp.arange(seq, dtype=F32)[:, None] * inv_freq[None, :]
    lane = jnp.arange(LANE)
    within = lane % DIFF_HEAD_DIM
    idx = within % half
    cos_l = jnp.cos(ang)[:, idx]
    sin_l = jnp.sin(ang)[:, idx]
    rot_c = jnp.where(within < ROT_DIM, cos_l, 1.0)
    rot_sa = jnp.where(within < half, -sin_l, 0.0)
    rot_sb = jnp.where((within >= half) & (within < ROT_DIM), sin_l, 0.0)
    return rot_c, rot_sa, rot_sb


def _lane_row(vals, offset):
    return jnp.zeros((1, LANE), F32).at[0, offset:offset + vals.shape[0]].set(vals.astype(F32))


def kernel(x, mix_norm_w, ffn_norm_w, final_norm_w, w_in_even, conv_w, conv_b, conv_ln_w, conv_ln_b, lambda_q1, lambda_k1, lambda_q2, lambda_k2, diff_subln_w, gdn_conv_w, w_in_odd, gdn_a_log, gdn_dt_bias, gdn_norm_w, fox_forget_bias, w_out, w_gate, w_up, w_down):
    batch, seq, d = x.shape
    m = batch * seq
    xf = x.reshape(m, d).astype(F32)
    rot_c, rot_sa, rot_sb = _rotary_tables(seq)
    row = lambda v: v.reshape(1, -1).astype(F32)

    for l in range(DEPTH):
        i = l // 2
        nw = row(mix_norm_w[l])
        if l % 2 == 0:
            lambda_init = 0.8 - 0.6 * math.exp(-0.3 * l)
            conv_in, qkv = _even_inproj(xf, nw, w_in_even[i].astype(BF16), rot_c, rot_sa, rot_sb, seq)
            cw = jnp.zeros((CONV_HALO, CONV_CH), F32).at[:CONV_WIDTH].set(conv_w[i].astype(F32))
            ya = _conv_module(conv_in, cw, row(conv_b[i]), row(conv_ln_w[i]), row(conv_ln_b[i]), batch, seq)
            yb = _diff_attention(qkv, row(lambda_q1[i]), row(lambda_k1[i]), row(lambda_q2[i]), row(lambda_k2[i]),
                                 row(diff_subln_w[i]), lambda_init, batch, seq)
        else:
            w = w_in_odd[i]
            o_fd = 4 * GDN_W + 2 * GDN_HEADS + 3 * FOX_W
            w_small = jnp.zeros((d, LANE), w.dtype)
            w_small = w_small.at[:, SM_F:SM_F + 4].set(w[:, o_fd:o_fd + 4])
            w_small = w_small.at[:, SM_B:SM_B + 4].set(w[:, 4 * GDN_W:4 * GDN_W + 4])
            w_small = w_small.at[:, SM_A:SM_A + 4].set(w[:, 4 * GDN_W + 4:4 * GDN_W + 8])
            w_re = jnp.concatenate([w[:, :4 * GDN_W], w[:, 4 * GDN_W + 8:o_fd], w_small], axis=1).astype(BF16)
            cw = jnp.zeros((SUBLANE, 3 * GDN_W), F32).at[:GDN_CONV].set(gdn_conv_w[i].astype(F32))
            gdn_in, small, fox = _odd_inproj(xf, nw, w_re, cw, seq)
            ya, c_col, c_row = _gdn(gdn_in, small, _lane_row(gdn_a_log[i], SM_A), _lane_row(gdn_dt_bias[i], SM_A),
                                    _lane_row(fox_forget_bias[i], SM_F), row(gdn_norm_w[i]), batch, seq)
            yb = _fox_attention(fox, c_col, c_row, batch, seq)
        wo = w_out[l].astype(BF16)
        half = wo.shape[0] // 2
        xf = _outproj_ffn(xf, ya, yb, wo[:half], wo[half:], row(ffn_norm_w[l]), w_gate[l].astype(BF16),
                          w_up[l].astype(BF16), w_down[l].astype(BF16), row(final_norm_w), l == DEPTH - 1)
    return xf.reshape(batch, seq, d).astype(x.dtype)
```
